```python
import jax, jax.numpy as jnp
from jax import lax
import numpy as np

D_MODEL = 1024
BATCH = 8
SEQ = 2048
DEPTH = 2
DEC_BATCH = 128
DEC_SEQ = 4
PAST_LEN = 16384
PAGE_SIZE = 128

D_FF = 2816
ML_HEADS = 4
ML_DK = 128
ML_DV = D_MODEL // ML_HEADS
GLA_HEADS = 4
GLA_DK = 128
GLA_DV = D_MODEL // GLA_HEADS
GLA_RANK = 16
GLA_TAU = 16.0
LRU_WIDTH = D_MODEL
LRU_BLOCKS = 8
LRU_BW = LRU_WIDTH // LRU_BLOCKS
LRU_C = 8.0
CONV_W = 4
CHUNK = 64
N_BRANCH = 3
EPS = 1e-6
ML_QK = ML_HEADS * ML_DK
GLA_QK = GLA_HEADS * GLA_DK
SPLITS = (ML_QK, ML_QK, D_MODEL, D_MODEL, 2 * ML_HEADS, GLA_QK, GLA_QK, D_MODEL, D_MODEL, GLA_RANK, LRU_WIDTH, LRU_WIDTH, N_BRANCH * D_MODEL)
D_IN = ML_QK * 2 + D_MODEL * 2 + 2 * ML_HEADS + GLA_QK * 2 + D_MODEL * 2 + GLA_RANK + LRU_WIDTH * 2 + N_BRANCH * D_MODEL

kernel_name = 'hybrid_mlstm_gla_rglru_macaron_step'

F32 = jnp.float32


def _rmsnorm(x, g):
    xf = x.astype(F32)
    y = xf * lax.rsqrt(jnp.mean(xf * xf, axis=-1, keepdims=True) + EPS)
    return (y * g.astype(F32)).astype(x.dtype)


def _swiglu(xn, w_in, w_out):
    gu = xn @ w_in
    g, u = gu[..., :D_FF], gu[..., D_FF:]
    return (jax.nn.silu(g) * u) @ w_out


def _split_cols(a):
    return jnp.split(a, np.cumsum(SPLITS)[:-1].tolist(), axis=-1)


def _heads(a, n):
    B, T, _ = a.shape
    return a.reshape(B, T, n, -1).transpose(0, 2, 1, 3)


def _chunk_len(T):
    return CHUNK if T % CHUNK == 0 else T


def _head_layernorm(h, g):
    B, H, T, d = h.shape
    h = h.transpose(0, 2, 1, 3)
    mu = jnp.mean(h, axis=-1, keepdims=True)
    hc = h - mu
    h = hc * lax.rsqrt(jnp.mean(hc * hc, axis=-1, keepdims=True) + EPS)
    return h.reshape(B, T, H * d) * g.astype(F32)


def _head_rmsnorm(h, g):
    B, H, T, d = h.shape
    h = h.transpose(0, 2, 1, 3)
    h = h * lax.rsqrt(jnp.mean(h * h, axis=-1, keepdims=True) + EPS)
    return h.reshape(B, T, H * d) * g.astype(F32)


def _mlstm(q, k, v, log_i, log_f, C0, n0, m0):
    B, H, T, _ = q.shape
    L = _chunk_len(T)
    nc = T // L
    causal = jnp.tril(jnp.ones((L, L), dtype=bool))

    def to_chunks(a):
        return jnp.moveaxis(a.reshape(B, H, nc, L, *a.shape[3:]), 2, 0)

    def step(carry, inp):
        C, n, m = carry
        qc, kc, vc, ic, fc = inp
        b = jnp.cumsum(fc, axis=-1)
        D = jnp.where(causal, b[..., :, None] - b[..., None, :] + ic[..., None, :], -jnp.inf)
        m_inter = b + m[..., None]
        m_t = jnp.maximum(m_inter, jnp.max(D, axis=-1))
        s_inter = jnp.exp(m_inter - m_t)
        w = jnp.einsum('bhtd,bhsd->bhts', qc, kc) * jnp.exp(D - m_t[..., None])
        num = jnp.einsum('bhts,bhsv->bhtv', w, vc) + s_inter[..., None] * jnp.einsum('bhvd,bhtd->bhtv', C, qc)
        den = jnp.sum(w, axis=-1) + s_inter * jnp.einsum('bhd,bhtd->bht', n, qc)
        h = num / jnp.maximum(jnp.abs(den), jnp.exp(-m_t))[..., None]
        g = b[..., -1:] - b + ic
        m_new = jnp.maximum(b[..., -1] + m, jnp.max(g, axis=-1))
        wk = jnp.exp(g - m_new[..., None])
        decay = jnp.exp(b[..., -1] + m - m_new)
        C_new = decay[..., None, None] * C + jnp.einsum('bhsv,bhsd->bhvd', vc * wk[..., None], kc)
        n_new = decay[..., None] * n + jnp.einsum('bhs,bhsd->bhd', wk, kc)
        return (C_new, n_new, m_new), h

    carry0 = (C0.astype(F32), n0.astype(F32), m0.astype(F32))
    xs = (to_chunks(q), to_chunks(k), to_chunks(v), to_chunks(log_i), to_chunks(log_f))
    (C1, n1, m1), h = lax.scan(step, carry0, xs)
    h = jnp.moveaxis(h, 0, 2).reshape(B, H, T, -1)
    return h, C1, n1, m1


def _gla(q, k, v, log_a, S0):
    B, H, T, _ = q.shape
    L = _chunk_len(T)
    nc = T // L
    causal = jnp.tril(jnp.ones((L, L), dtype=bool))

    def to_chunks(a):
        return jnp.moveaxis(a.reshape(B, H, nc, L, *a.shape[3:]), 2, 0)

    def step(S, inp):
        qc, kc, vc, ac = inp
        b = jnp.cumsum(ac, axis=2)
        diff = jnp.where(causal[:, :, None], b[:, :, :, None, :] - b[:, :, None, :, :], -jnp.inf)
        A = jnp.einsum('bhtd,bhsd,bhtsd->bhts', qc, kc, jnp.exp(diff))
        o = jnp.einsum('bhts,bhsv->bhtv', A, vc) + jnp.einsum('bhtd,bhdv->bhtv', qc * jnp.exp(b), S)
        bL = b[:, :, -1]
        S_new = jnp.exp(bL)[..., None] * S + jnp.einsum('bhsd,bhsv->bhdv', kc * jnp.exp(bL[:, :, None] - b), vc)
        return S_new, o

    xs = (to_chunks(q), to_chunks(k), to_chunks(v), to_chunks(log_a))
    S1, o = lax.scan(step, S0.astype(F32), xs)
    o = jnp.moveaxis(o, 0, 2).reshape(B, H, T, -1)
    return o, S1


def _causal_conv(x, buf, w, b):
    T = x.shape[1]
    xp = jnp.concatenate([buf.astype(x.dtype), x], axis=1)
    y = b + sum(xp[:, j:j + T] * w[j] for j in range(CONV_W))
    return y, xp[:, -(CONV_W - 1):]


def _rglru(xc, w_a, b_a, w_i, b_i, lam, h0, pos):
    B, T, C = xc.shape
    xb = xc.reshape(B, T, LRU_BLOCKS, LRU_BW)
    r = jax.nn.sigmoid(jnp.einsum('btnc,ncd->btnd', xb, w_a.astype(F32)).reshape(B, T, C) + b_a)
    i = jax.nn.sigmoid(jnp.einsum('btnc,ncd->btnd', xb, w_i.astype(F32)).reshape(B, T, C) + b_i)
    log_a = -LRU_C * r * jax.nn.softplus(-lam.astype(F32))
    a = jnp.exp(log_a)
    mult = jnp.sqrt(-jnp.expm1(2.0 * log_a))
    mult = jnp.where((pos == 0)[None, :, None], 1.0, mult)
    u = mult * i * xc
    u = u.at[:, 0].add(a[:, 0] * h0.astype(F32))

    def comb(l, rr):
        return (l[0] * rr[0], rr[0] * l[1] + rr[1])

    _, h = lax.associative_scan(comb, (a, u), axis=1)
    return h, h[:, -1]


def _layer(x, pos, state, p):
    C0, n0, m0, S0, h0, cb0 = state
    B, T, _ = x.shape
    x = x + 0.5 * _swiglu(_rmsnorm(x, p['norm_ffn1']), p['w_ffn1_in'], p['w_ffn1_out'])
    xn = _rmsnorm(x, p['norm_mix'])
    (mq, mk, mv, mo, mg_if, gq, gk, gv, gg, glr, lx, ly, mgate) = _split_cols(xn @ p['w_in'])
    q = _heads(mq, ML_HEADS).astype(F32)
    k = _heads(mk, ML_HEADS).astype(F32) * (ML_DK ** -0.5)
    v = _heads(mv, ML_HEADS).astype(F32)
    gates = mg_if.astype(F32) + p['b_ml_if'].astype(F32)
    log_i = gates[..., :ML_HEADS].transpose(0, 2, 1)
    log_f = jax.nn.log_sigmoid(gates[..., ML_HEADS:]).transpose(0, 2, 1)
    h_ml, C1, n1, m1 = _mlstm(q, k, v, log_i, log_f, C0, n0, m0)
    y_ml = _head_layernorm(h_ml, p['g_ml_norm']) * jax.nn.sigmoid(mo.astype(F32))
    gq_ = _heads(gq, GLA_HEADS).astype(F32) * (GLA_DK ** -0.5)
    gk_ = _heads(gk, GLA_HEADS).astype(F32)
    gv_ = _heads(gv, GLA_HEADS).astype(F32)
    log_a = jax.nn.log_sigmoid((glr @ p['w_gla_lr2']).astype(F32) + p['b_gla_gate'].astype(F32)) / GLA_TAU
    o_gla, S1 = _gla(gq_, gk_, gv_, _heads(log_a, GLA_HEADS), S0)
    y_gla = _head_rmsnorm(o_gla, p['g_gla_norm']) * jax.nn.silu(gg.astype(F32))
    xc, cb1 = _causal_conv(lx, cb0, p['w_conv'], p['b_conv'])
    h_lru, h1 = _rglru(xc.astype(F32), p['w_lru_a'], p['b_lru_a'].astype(F32), p['w_lru_i'], p['b_lru_i'].astype(F32), p['lru_lambda'], h0, pos)
    y_lru = h_lru * jax.nn.gelu(ly.astype(F32))
    gm = jax.nn.sigmoid(mgate.astype(F32)).reshape(B, T, N_BRANCH, D_MODEL)
    merged = gm[:, :, 0] * y_ml + gm[:, :, 1] * y_gla + gm[:, :, 2] * y_lru
    x = x + merged.astype(x.dtype) @ p['w_out']
    x = x + 0.5 * _swiglu(_rmsnorm(x, p['norm_ffn2']), p['w_ffn2_in'], p['w_ffn2_out'])
    return x, (C1, n1, m1, S1, h1, cb1)


def setup_inputs(seed: int = 0) -> dict:
    key = jax.random.key(seed)
    ks = iter(jax.random.split(key, 48))

    def nrm(shape, s):
        return jax.random.normal(next(ks), shape, F32) * s

    Ld = DEPTH
    d = D_MODEL
    u = jax.random.uniform(next(ks), (Ld, LRU_WIDTH), F32, minval=0.9, maxval=0.999)
    sa = u ** (1.0 / LRU_C)
    lam = jnp.log(sa) - jnp.log1p(-sa)
    b_if = jnp.concatenate([nrm((Ld, ML_HEADS), 0.1), jnp.linspace(3.0, 6.0, ML_HEADS, dtype=F32)[None] + nrm((Ld, ML_HEADS), 0.1)], axis=-1)
    return {
        'x_prompt': nrm((BATCH, SEQ, d), 1.0),
        'x_sample': nrm((DEC_BATCH, DEC_SEQ, d), 1.0),
        'state_mlstm_C': nrm((Ld, DEC_BATCH, ML_HEADS, ML_DV, ML_DK), 0.05),
        'state_mlstm_n': jnp.abs(nrm((Ld, DEC_BATCH, ML_HEADS, ML_DK), 0.5)),
        'state_mlstm_m': nrm((Ld, DEC_BATCH, ML_HEADS), 1.0),
        'state_gla_S': nrm((Ld, DEC_BATCH, GLA_HEADS, GLA_DK, GLA_DV), 0.5),
        'state_lru_h': nrm((Ld, DEC_BATCH, LRU_WIDTH), 0.5),
        'state_lru_conv': nrm((Ld, DEC_BATCH, CONV_W - 1, LRU_WIDTH), 1.0),
        'norm_ffn1': 1.0 + nrm((Ld, d), 0.02),
        'w_ffn1_in': nrm((Ld, d, 2 * D_FF), d ** -0.5),
        'w_ffn1_out': nrm((Ld, D_FF, d), D_FF ** -0.5),
        'norm_mix': 1.0 + nrm((Ld, d), 0.02),
        'w_in': nrm((Ld, d, D_IN), d ** -0.5),
        'b_ml_if': b_if,
        'g_ml_norm': 1.0 + nrm((Ld, D_MODEL), 0.02),
        'w_gla_lr2': nrm((Ld, GLA_RANK, GLA_QK), GLA_RANK ** -0.5),
        'b_gla_gate': nrm((Ld, GLA_QK), 0.1),
        'g_gla_norm': 1.0 + nrm((Ld, D_MODEL), 0.02),
        'w_conv': nrm((Ld, CONV_W, LRU_WIDTH), CONV_W ** -0.5),
        'b_conv': nrm((Ld, LRU_WIDTH), 0.02),
        'w_lru_a': nrm((Ld, LRU_BLOCKS, LRU_BW, LRU_BW), LRU_BW ** -0.5),
        'b_lru_a': nrm((Ld, LRU_WIDTH), 0.1),
        'w_lru_i': nrm((Ld, LRU_BLOCKS, LRU_BW, LRU_BW), LRU_BW ** -0.5),
        'b_lru_i': nrm((Ld, LRU_WIDTH), 0.1),
        'lru_lambda': lam,
        'w_out': nrm((Ld, d, d), d ** -0.5),
        'norm_ffn2': 1.0 + nrm((Ld, d), 0.02),
        'w_ffn2_in': nrm((Ld, d, 2 * D_FF), d ** -0.5),
        'w_ffn2_out': nrm((Ld, D_FF, d), D_FF ** -0.5),
        'norm_final': 1.0 + nrm((d,), 0.02),
    }


def reference(x_prompt, x_sample, state_mlstm_C, state_mlstm_n, state_mlstm_m, state_gla_S, state_lru_h, state_lru_conv,
              norm_ffn1, w_ffn1_in, w_ffn1_out, norm_mix, w_in, b_ml_if, g_ml_norm, w_gla_lr2, b_gla_gate, g_gla_norm,
              w_conv, b_conv, w_lru_a, b_lru_a, w_lru_i, b_lru_i, lru_lambda, w_out, norm_ffn2, w_ffn2_in, w_ffn2_out,
              norm_final):
    Bp, Tp, _ = x_prompt.shape
    Ts = x_sample.shape[1]
    pos_p = jnp.arange(Tp)
    pos_s = PAST_LEN + jnp.arange(Ts)
    zero_p = (jnp.zeros((Bp, ML_HEADS, ML_DV, ML_DK), F32), jnp.zeros((Bp, ML_HEADS, ML_DK), F32),
              jnp.zeros((Bp, ML_HEADS), F32), jnp.zeros((Bp, GLA_HEADS, GLA_DK, GLA_DV), F32),
              jnp.zeros((Bp, LRU_WIDTH), F32), jnp.zeros((Bp, CONV_W - 1, LRU_WIDTH), x_prompt.dtype))
    yp, ys = x_prompt, x_sample
    new_p, new_s = [], []
    for l in range(DEPTH):
        p = {'norm_ffn1': norm_ffn1[l], 'w_ffn1_in': w_ffn1_in[l], 'w_ffn1_out': w_ffn1_out[l],
             'norm_mix': norm_mix[l], 'w_in': w_in[l], 'b_ml_if': b_ml_if[l], 'g_ml_norm': g_ml_norm[l],
             'w_gla_lr2': w_gla_lr2[l], 'b_gla_gate': b_gla_gate[l], 'g_gla_norm': g_gla_norm[l],
             'w_conv': w_conv[l], 'b_conv': b_conv[l], 'w_lru_a': w_lru_a[l], 'b_lru_a': b_lru_a[l],
             'w_lru_i': w_lru_i[l], 'b_lru_i': b_lru_i[l], 'lru_lambda': lru_lambda[l], 'w_out': w_out[l],
             'norm_ffn2': norm_ffn2[l], 'w_ffn2_in': w_ffn2_in[l], 'w_ffn2_out': w_ffn2_out[l]}
        yp, st_p = _layer(yp, pos_p, zero_p, p)
        st_in = (state_mlstm_C[l], state_mlstm_n[l], state_mlstm_m[l], state_gla_S[l], state_lru_h[l], state_lru_conv[l])
        ys, st_s = _layer(ys, pos_s, st_in, p)
        new_p.append(st_p)
        new_s.append(st_s)
    y_prompt = _rmsnorm(yp, norm_final)
    y_sample = _rmsnorm(ys, norm_final)
    p_C = jnp.stack([s[0] for s in new_p])
    p_n = jnp.stack([s[1] for s in new_p])
    p_m = jnp.stack([s[2] for s in new_p])
    p_S = jnp.stack([s[3] for s in new_p])
    p_h = jnp.stack([s[4] for s in new_p])
    p_conv = jnp.stack([s[5] for s in new_p])
    s_C = jnp.stack([s[0] for s in new_s])
    s_n = jnp.stack([s[1] for s in new_s])
    s_m = jnp.stack([s[2] for s in new_s])
    s_S = jnp.stack([s[3] for s in new_s])
    s_h = jnp.stack([s[4] for s in new_s])
    s_conv = jnp.stack([s[5] for s in new_s])
    return (y_prompt, y_sample, p_C, p_n, p_m, p_S, p_h, p_conv, s_C, s_n, s_m, s_S, s_h, s_conv)
```

```python
import functools

import jax
import jax.numpy as jnp
from jax import lax
from jax.experimental import pallas as pl
from jax.experimental.pallas import tpu as pltpu

F32 = jnp.float32
BF16 = jnp.bfloat16

D_MODEL = 1024
D_FF = 2816
HEADS = 4
DK = 128
DV = 256
GLA_RANK = 16
GLA_TAU = 16.0
LRU_BLOCKS = 8
LRU_BW = 128
LRU_C = 8.0
CONV_W = 4
EPS = 1e-6
NEG = -1e30

LANES = 128
SUBLANES = 8

P_COLS = 11264
C512_MQ, C512_MK, C512_GQ, C512_GK = 0, 1, 4, 5
C1K_MV, C1K_GV, C1K_LX, C1K_MO, C1K_GG, C1K_LY, C1K_MG0 = 1, 3, 4, 5, 6, 7, 8

ML_CHUNK = 256
GLA_CHUNK = 128
MIX_TB = 256
SAMPLE_NB = 8
WIN = 8


def _cparams(n_axes, vmem_mib):
    return pltpu.CompilerParams(dimension_semantics=("arbitrary",) * n_axes,
                                vmem_limit_bytes=vmem_mib * 1024 * 1024)


def _pick_tile(n, target, mult):
    best = None
    for t in range(mult, min(n, target) + 1, mult):
        if n % t == 0:
            best = t
    assert best is not None, (n, target, mult)
    return best


def _sigmoid(x):
    return 1.0 / (1.0 + jnp.exp(-x))


def _log_sigmoid(x):
    return jnp.minimum(x, 0.0) - jnp.log(1.0 + jnp.exp(-jnp.abs(x)))


def _softplus(x):
    return jnp.maximum(x, 0.0) + jnp.log(1.0 + jnp.exp(-jnp.abs(x)))


def _gelu_tanh(x):
    return 0.5 * x * (1.0 + jnp.tanh(0.7978845608028654 * (x + 0.044715 * (x * x * x))))


def _rms(x, g):
    return x * lax.rsqrt(jnp.mean(x * x, axis=-1, keepdims=True) + EPS) * g


def _nn(a, b):
    return jnp.dot(a, b, preferred_element_type=F32)


def _nt(a, b):
    return lax.dot_general(a, b, (((1,), (1,)), ((), ())), preferred_element_type=F32)


def _tn(a, b):
    return lax.dot_general(a, b, (((0,), (0,)), ((), ())), preferred_element_type=F32)


def _split3(x):
    hi = x.astype(BF16)
    r1 = x - hi.astype(F32)
    mid = r1.astype(BF16)
    lo = (r1 - mid.astype(F32)).astype(BF16)
    return hi, mid, lo


def _sel_left(sel, x):
    hi, mid, lo = _split3(x)
    return _nn(sel, hi) + _nn(sel, mid) + _nn(sel, lo)


def _sel_right(x, sel):
    hi, mid, lo = _split3(x)
    return _nn(hi, sel) + _nn(mid, sel) + _nn(lo, sel)


def _tri_masks(L):
    rowi = lax.broadcasted_iota(jnp.int32, (L, L), 0)
    coli = lax.broadcasted_iota(jnp.int32, (L, L), 1)
    causal = rowi >= coli
    tril = jnp.where(causal, 1.0, 0.0).astype(BF16)
    triu = jnp.where(rowi <= coli, 1.0, 0.0).astype(BF16)
    return rowi, coli, causal, tril, triu


def _ffn_body(x_ref, g_ref, wg_ref, wu_ref, wo_ref, gn_ref, *rest, nf, final):
    if final:
        y_ref, xn_s, acc_s = rest
    else:
        y_ref, xn_ref, xn_s, acc_s = rest
    j = pl.program_id(1)

    @pl.when(j == 0)
    def _():
        xn_s[...] = _rms(x_ref[...], g_ref[...]).astype(BF16)
        acc_s[...] = jnp.zeros_like(acc_s)

    xn = xn_s[...]
    g = _nn(xn, wg_ref[...])
    u = _nn(xn, wu_ref[...])
    a = (g * _sigmoid(g)) * u
    acc_s[...] += _nn(a.astype(BF16), wo_ref[...])

    @pl.when(j == nf - 1)
    def _():
        xo = x_ref[...] + 0.5 * acc_s[...]
        if final:
            y_ref[...] = _rms(xo, gn_ref[...])
        else:
            y_ref[...] = xo
            xn_ref[...] = _rms(xo, gn_ref[...]).astype(BF16)


def _ffn(x, g, w_in, w_out, g_next, *, final):
    nt_rows = x.shape[0]
    tm = _pick_tile(nt_rows, 768, 16)
    tf = 256
    nf = D_FF // tf
    grid = (nt_rows // tm, nf)
    in_specs = [
        pl.BlockSpec((tm, D_MODEL), lambda i, j: (i, 0)),
        pl.BlockSpec((1, D_MODEL), lambda i, j: (0, 0)),
        pl.BlockSpec((D_MODEL, tf), lambda i, j: (0, j)),
        pl.BlockSpec((D_MODEL, tf), lambda i, j: (0, j + nf)),
        pl.BlockSpec((tf, D_MODEL), lambda i, j: (j, 0)),
        pl.BlockSpec((1, D_MODEL), lambda i, j: (0, 0)),
    ]
    row_spec = pl.BlockSpec((tm, D_MODEL), lambda i, j: (i, 0))
    if final:
        out_shape = jax.ShapeDtypeStruct((nt_rows, D_MODEL), F32)
        out_specs = row_spec
    else:
        out_shape = (jax.ShapeDtypeStruct((nt_rows, D_MODEL), F32), jax.ShapeDtypeStruct((nt_rows, D_MODEL), BF16))
        out_specs = (row_spec, row_spec)
    return pl.pallas_call(
        functools.partial(_ffn_body, nf=nf, final=final),
        grid=grid, in_specs=in_specs, out_specs=out_specs, out_shape=out_shape,
        scratch_shapes=[pltpu.VMEM((tm, D_MODEL), BF16), pltpu.VMEM((tm, D_MODEL), F32)],
        compiler_params=_cparams(2, 48), name="ffn_final" if final else "ffn",
    )(x, g.reshape(1, D_MODEL), w_in, w_in, w_out, g_next.reshape(1, D_MODEL))


def _proj_body(xn_ref, w_ref, ws_ref, p_ref, g_ref):
    xn = xn_ref[...]
    p_ref[...] = _nn(xn, w_ref[...]).astype(BF16)

    @pl.when(pl.program_id(1) == 0)
    def _():
        g_ref[...] = _nn(xn, ws_ref[...])


def _proj(xn, w_big, w_small):
    nt_rows = xn.shape[0]
    tm = _pick_tile(nt_rows, 1536, 16)
    tn = 1024
    grid = (nt_rows // tm, P_COLS // tn)
    return pl.pallas_call(
        _proj_body, grid=grid,
        in_specs=[pl.BlockSpec((tm, D_MODEL), lambda i, j: (i, 0)),
                  pl.BlockSpec((D_MODEL, tn), lambda i, j: (0, j)),
                  pl.BlockSpec((D_MODEL, LANES), lambda i, j: (0, 0))],
        out_specs=(pl.BlockSpec((tm, tn), lambda i, j: (i, j)),
                   pl.BlockSpec((tm, LANES), lambda i, j: (i, 0))),
        out_shape=(jax.ShapeDtypeStruct((nt_rows, P_COLS), BF16), jax.ShapeDtypeStruct((nt_rows, LANES), F32)),
        compiler_params=_cparams(2, 40), name="proj",
    )(xn, w_big, w_small)


def _merge_body(x_ref, mo_ref, gg_ref, ly_ref, m0_ref, m1_ref, m2_ref, yml_ref, ygla_ref, hlru_ref, w_ref, o_ref):
    f = lambda r: r[...].astype(F32)
    y_ml = f(yml_ref) * _sigmoid(f(mo_ref))
    y_gla = f(ygla_ref) * (f(gg_ref) * _sigmoid(f(gg_ref)))
    y_lru = f(hlru_ref) * _gelu_tanh(f(ly_ref))
    merged = _sigmoid(f(m0_ref)) * y_ml + _sigmoid(f(m1_ref)) * y_gla + _sigmoid(f(m2_ref)) * y_lru
    o_ref[...] = x_ref[...] + _nn(merged.astype(BF16), w_ref[...])


def _merge(x, p, y_ml, y_gla, h_lru, w_out):
    nt_rows = x.shape[0]
    tm = _pick_tile(nt_rows, 256, 16)
    row = lambda c: pl.BlockSpec((tm, D_MODEL), lambda i, c=c: (i, c))
    return pl.pallas_call(
        _merge_body, grid=(nt_rows // tm,),
        in_specs=[row(0), row(C1K_MO), row(C1K_GG), row(C1K_LY), row(C1K_MG0), row(C1K_MG0 + 1), row(C1K_MG0 + 2),
                  row(0), row(0), row(0), pl.BlockSpec((D_MODEL, D_MODEL), lambda i: (0, 0))],
        out_specs=row(0),
        out_shape=jax.ShapeDtypeStruct((nt_rows, D_MODEL), F32),
        compiler_params=_cparams(1, 48), name="merge",
    )(x, p, p, p, p, p, p, y_ml, y_gla, h_lru, w_out)


def _mlstm_chunk(q, k, v, gi, lf, states, gnorm, L):
    _, _, causal, tril, triu = _tri_masks(L)
    b_cols = _sel_left(tril, lf)
    gi_t = gi.T[0:SUBLANES]
    lf_t = lf.T[0:SUBLANES]
    b_rows = _sel_right(lf_t, triu)
    scale = DK ** -0.5
    ys, new_states = [], []
    for h in range(HEADS):
        c_mat, n_vec, m_prev = states[h]
        qh = q[:, h * DK:(h + 1) * DK]
        kh = k[:, h * DK:(h + 1) * DK]
        vh = v[:, h * DV:(h + 1) * DV]
        i_col = gi[:, h:h + 1]
        b_col = b_cols[:, HEADS + h:HEADS + h + 1]
        c_row = gi_t[h:h + 1, :] - b_rows[HEADS + h:HEADS + h + 1, :]
        dm = jnp.where(causal, c_row, NEG)
        big_m = jnp.maximum(m_prev, jnp.max(dm, axis=-1, keepdims=True))
        pmat = jnp.exp(dm - big_m)
        w = (_nt(qh, kh) * scale) * pmat
        s_inter = jnp.exp(m_prev - big_m)
        num = _nn(w.astype(BF16), vh) + s_inter * _nt(qh, c_mat.astype(BF16))
        qn = jnp.sum(qh.astype(F32) * n_vec, axis=-1, keepdims=True)
        den = jnp.sum(w, axis=-1, keepdims=True) + s_inter * qn
        hh = num / jnp.maximum(jnp.abs(den), jnp.exp(-(b_col + big_m)))
        mu = jnp.mean(hh, axis=-1, keepdims=True)
        hc = hh - mu
        ys.append(hc * lax.rsqrt(jnp.mean(hc * hc, axis=-1, keepdims=True) + EPS) * gnorm[:, h * DV:(h + 1) * DV])
        m_last = big_m[L - 1:L, :]
        wk = jnp.exp((i_col - b_col) - m_last)
        decay = jnp.exp(m_prev - m_last)
        kw = kh.astype(F32) * (wk * scale)
        c_new = decay * c_mat + _tn(vh, kw.astype(BF16))
        n_new = decay * n_vec + jnp.sum(kw, axis=0, keepdims=True)
        m_new = b_col[L - 1:L, :] + m_last
        new_states.append((c_new, n_new, m_new))
    return ys, new_states


def _mlstm_prompt_body(q_ref, k_ref, v_ref, g_ref, bias_ref, gn_ref, y_ref, c_ref, n_ref, m_ref, *, L, NC):
    @pl.when(pl.program_id(1) == 0)
    def _():
        c_ref[...] = jnp.zeros_like(c_ref)
        n_ref[...] = jnp.zeros_like(n_ref)
        m_ref[...] = jnp.zeros_like(m_ref)

    gates = g_ref[...] + bias_ref[...]
    lf_all = _log_sigmoid(gates)
    gnorm = gn_ref[...]
    for c in range(NC):
        sl = slice(c * L, (c + 1) * L)
        states = [(c_ref[0, h], n_ref[0, h:h + 1, :], m_ref[0, h:h + 1, 0:1]) for h in range(HEADS)]
        ys, new = _mlstm_chunk(q_ref[sl, :], k_ref[sl, :], v_ref[sl, :], gates[sl], lf_all[sl], states, gnorm, L)
        for h in range(HEADS):
            y_ref[sl, h * DV:(h + 1) * DV] = ys[h].astype(BF16)
            c_ref[0, h] = new[h][0]
            n_ref[0, h:h + 1, :] = new[h][1]
            m_ref[0, h:h + 1, :] = jnp.broadcast_to(new[h][2], (1, LANES))


def _mlstm_prompt(p, gates, bias, gnorm, *, nb, t_len, nt_rows):
    tb = MIX_TB
    nblk = t_len // tb
    rows = lambda width, col: pl.BlockSpec((tb, width), lambda b, t, col=col: (b * nblk + t, col))
    vec = lambda width: pl.BlockSpec((1, width), lambda b, t: (0, 0))
    return pl.pallas_call(
        functools.partial(_mlstm_prompt_body, L=ML_CHUNK, NC=tb // ML_CHUNK),
        grid=(nb, nblk),
        in_specs=[rows(512, C512_MQ), rows(512, C512_MK), rows(1024, C1K_MV), rows(LANES, 0), vec(LANES), vec(D_MODEL)],
        out_specs=(rows(D_MODEL, 0),
                   pl.BlockSpec((1, HEADS, DV, DK), lambda b, t: (b, 0, 0, 0)),
                   pl.BlockSpec((1, HEADS, DK), lambda b, t: (b, 0, 0)),
                   pl.BlockSpec((1, HEADS, LANES), lambda b, t: (b, 0, 0))),
        out_shape=(jax.ShapeDtypeStruct((nt_rows, D_MODEL), BF16),
                   jax.ShapeDtypeStruct((nb, HEADS, DV, DK), F32),
                   jax.ShapeDtypeStruct((nb, HEADS, DK), F32),
                   jax.ShapeDtypeStruct((nb, HEADS, LANES), F32)),
        compiler_params=_cparams(2, 40), name="mlstm_prompt",
    )(p, p, p, gates, bias, gnorm)


def _window_masks():
    rowl = lax.broadcasted_iota(jnp.int32, (WIN, 1), 0)
    return (rowl < WIN // 2, rowl >= WIN // 2)


def _mlstm_sample_body(q_ref, k_ref, v_ref, g_ref, bias_ref, gn_ref, c0_ref, n0_ref, m0_ref, *rest, NB, has_prev):
    if has_prev:
        _, _, y_ref, c_ref, n_ref, m_ref = rest
    else:
        _, y_ref, c_ref, n_ref, m_ref = rest
    gates = g_ref[...] + bias_ref[...]
    lf_all = _log_sigmoid(gates)
    gnorm = gn_ref[...]
    q_all = q_ref[...].astype(F32)
    k_all = k_ref[...].astype(F32)
    v_all = v_ref[...].astype(F32)
    valid = _window_masks()
    y_rows = [[] for _ in range(HEADS)]
    for w in range(NB // 2):
        sl = slice(WIN * w, WIN * (w + 1))
        qw, kw, vw = q_all[sl].astype(BF16), k_all[sl].astype(BF16), v_all[sl].astype(BF16)
        yw = [None] * HEADS
        for par in range(2):
            b = 2 * w + par
            gi = jnp.where(valid[par], gates[sl], NEG)
            lf = jnp.where(valid[par], lf_all[sl], 0.0)
            states = [(c0_ref[0, b, h], n0_ref[0, b, h:h + 1, :], m0_ref[0, b:b + 1, h:h + 1]) for h in range(HEADS)]
            ys, new = _mlstm_chunk(qw, kw, vw, gi, lf, states, gnorm, WIN)
            for h in range(HEADS):
                yw[h] = ys[h] if par == 0 else jnp.where(valid[1], ys[h], yw[h])
                c_ref[0, b, h] = new[h][0]
                n_ref[b, h:h + 1, :] = new[h][1]
                m_ref[b, h:h + 1, :] = jnp.broadcast_to(new[h][2], (1, LANES))
        for h in range(HEADS):
            y_rows[h].append(yw[h])
    for h in range(HEADS):
        y_ref[:, h * DV:(h + 1) * DV] = jnp.concatenate(y_rows[h], axis=0).astype(BF16)


def _mlstm_sample(p, gates, bias, gnorm, c0, n0, m0, y_prev, c_prev, *, layer, n_seq, row0):
    nbk = SAMPLE_NB
    rb = nbk * 4
    off = row0 // rb
    rows = lambda width, col: pl.BlockSpec((rb, width), lambda i, col=col: (off + i, col))
    vec = lambda width: pl.BlockSpec((1, width), lambda i: (0, 0))
    any_spec = pl.BlockSpec(memory_space=pl.ANY)
    has_prev = c_prev is not None
    in_specs = [rows(512, C512_MQ), rows(512, C512_MK), rows(1024, C1K_MV), rows(LANES, 0), vec(LANES), vec(D_MODEL),
                pl.BlockSpec((1, nbk, HEADS, DV, DK), lambda i: (layer, i, 0, 0, 0)),
                pl.BlockSpec((1, nbk, HEADS, DK), lambda i: (layer, i, 0, 0)),
                pl.BlockSpec((1, nbk, HEADS), lambda i: (layer, i, 0)),
                any_spec]
    args = [p, p, p, gates, bias, gnorm, c0, n0, m0, y_prev]
    aliases = {9: 0}
    if has_prev:
        in_specs.append(any_spec)
        args.append(c_prev)
        aliases[10] = 1
    return pl.pallas_call(
        functools.partial(_mlstm_sample_body, NB=nbk, has_prev=has_prev),
        grid=(n_seq // nbk,),
        in_specs=in_specs,
        out_specs=(rows(D_MODEL, 0),
                   pl.BlockSpec((1, nbk, HEADS, DV, DK), lambda i: (layer, i, 0, 0, 0)),
                   pl.BlockSpec((nbk, HEADS, DK), lambda i: (i, 0, 0)),
                   pl.BlockSpec((nbk, HEADS, LANES), lambda i: (i, 0, 0))),
        out_shape=(jax.ShapeDtypeStruct(y_prev.shape, BF16),
                   jax.ShapeDtypeStruct(c0.shape, F32),
                   jax.ShapeDtypeStruct((n_seq, HEADS, DK), F32),
                   jax.ShapeDtypeStruct((n_seq, HEADS, LANES), F32)),
        input_output_aliases=aliases,
        compiler_params=_cparams(1, 48), name="mlstm_sample",
    )(*args)


def _ref_rows(b, m, L):
    if m >= SUBLANES:
        nb = L // m
        b3 = b.reshape(nb, m, LANES)
        r = b3[:, m // 2 - 1:m // 2, :]
        return jnp.broadcast_to(r, (nb, m, LANES)).reshape(L, LANES)
    bt = b.reshape(L // SUBLANES, SUBLANES, LANES)
    sub = lax.broadcasted_iota(jnp.int32, bt.shape, 1)
    out = None
    for j in range(SUBLANES // m):
        idx = j * m + m // 2 - 1
        rj = jnp.broadcast_to(bt[:, idx:idx + 1, :], bt.shape)
        out = rj if out is None else jnp.where(sub >= j * m, rj, out)
    return out.reshape(L, LANES)


def _gla_chunk(q, k, v, la, states, gnorm, L, kvalid=None):
    rowi, coli, _, tril, _ = _tri_masks(L)
    xor_rc = rowi ^ coli
    row1 = lax.broadcasted_iota(jnp.int32, (L, 1), 0)
    scale = DK ** -0.5
    ys, new_states = [], []
    for h in range(HEADS):
        s_mat = states[h]
        qf = q[:, h * DK:(h + 1) * DK].astype(F32) * scale
        kf = k[:, h * DK:(h + 1) * DK].astype(F32)
        if kvalid is not None:
            kf = jnp.where(kvalid, kf, 0.0)
        vh = v[:, h * DV:(h + 1) * DV]
        b = _sel_left(tril, la[:, h * DK:(h + 1) * DK])
        a_mat = jnp.zeros((L, L), F32)
        m = L
        while m >= 2:
            ref = _ref_rows(b, m, L)
            second = (row1 & (m // 2)) != 0
            f = jnp.exp(jnp.where(second, b - ref, ref - b))
            qt = jnp.where(second, qf * f, 0.0).astype(BF16)
            kt = jnp.where(second, 0.0, kf * f).astype(BF16)
            am = _nt(qt, kt)
            if m < L:
                am = jnp.where(xor_rc < m, am, 0.0)
            a_mat = a_mat + am
            m //= 2
        diag = jnp.sum(qf * kf, axis=-1, keepdims=True)
        o = (_nn(a_mat.astype(BF16), vh) + diag * vh.astype(F32)
             + _nn((qf * jnp.exp(b)).astype(BF16), s_mat.astype(BF16)))
        ys.append(o * lax.rsqrt(jnp.mean(o * o, axis=-1, keepdims=True) + EPS) * gnorm[:, h * DV:(h + 1) * DV])
        b_last = b[L - 1:L, :]
        kd = (kf * jnp.exp(b_last - b)).astype(BF16)
        e_col = jnp.broadcast_to(jnp.exp(b_last), (SUBLANES, LANES)).T[:, 0:1]
        new_states.append(e_col * s_mat + _tn(kd, vh))
    return ys, new_states


def _gla_log_decay(gates, w2_ref, b2_ref):
    return _log_sigmoid(_nn(gates.astype(BF16), w2_ref[...]) + b2_ref[...]) * (1.0 / GLA_TAU)


def _gla_prompt_body(q_ref, k_ref, v_ref, g_ref, w2_ref, b2_ref, gn_ref, y_ref, s_ref, *, L, NC):
    @pl.when(pl.program_id(1) == 0)
    def _():
        s_ref[...] = jnp.zeros_like(s_ref)

    la_all = _gla_log_decay(g_ref[...], w2_ref, b2_ref)
    gnorm = gn_ref[...]
    for c in range(NC):
        sl = slice(c * L, (c + 1) * L)
        states = [s_ref[0, h] for h in range(HEADS)]
        ys, new = _gla_chunk(q_ref[sl, :], k_ref[sl, :], v_ref[sl, :], la_all[sl], states, gnorm, L)
        for h in range(HEADS):
            y_ref[sl, h * DV:(h + 1) * DV] = ys[h].astype(BF16)
            s_ref[0, h] = new[h]


def _gla_prompt(p, gates, w2, b2, gnorm, *, nb, t_len, nt_rows):
    tb = MIX_TB
    nblk = t_len // tb
    rows = lambda width, col: pl.BlockSpec((tb, width), lambda b, t, col=col: (b * nblk + t, col))
    vec = lambda width: pl.BlockSpec((1, width), lambda b, t: (0, 0))
    return pl.pallas_call(
        functools.partial(_gla_prompt_body, L=GLA_CHUNK, NC=tb // GLA_CHUNK),
        grid=(nb, nblk),
        in_specs=[rows(512, C512_GQ), rows(512, C512_GK), rows(1024, C1K_GV), rows(LANES, 0),
                  pl.BlockSpec((LANES, 512), lambda b, t: (0, 0)), vec(512), vec(D_MODEL)],
        out_specs=(rows(D_MODEL, 0), pl.BlockSpec((1, HEADS, DK, DV), lambda b, t: (b, 0, 0, 0))),
        out_shape=(jax.ShapeDtypeStruct((nt_rows, D_MODEL), BF16), jax.ShapeDtypeStruct((nb, HEADS, DK, DV), F32)),
        compiler_params=_cparams(2, 40), name="gla_prompt",
    )(p, p, p, gates, w2, b2, gnorm)


def _gla_sample_body(q_ref, k_ref, v_ref, g_ref, w2_ref, b2_ref, gn_ref, s0_ref, *rest, NB, has_prev):
    if has_prev:
        _, _, y_ref, s_ref = rest
    else:
        _, y_ref, s_ref = rest
    la_all = _gla_log_decay(g_ref[...], w2_ref, b2_ref)
    gnorm = gn_ref[...]
    q_all = q_ref[...].astype(F32)
    k_all = k_ref[...].astype(F32)
    v_all = v_ref[...].astype(F32)
    valid = _window_masks()
    y_rows = [[] for _ in range(HEADS)]
    for w in range(NB // 2):
        sl = slice(WIN * w, WIN * (w + 1))
        qw, kw, vw = q_all[sl].astype(BF16), k_all[sl].astype(BF16), v_all[sl].astype(BF16)
        yw = [None] * HEADS
        for par in range(2):
            b = 2 * w + par
            la = jnp.where(valid[par], la_all[sl], 0.0)
            states = [s0_ref[0, b, h] for h in range(HEADS)]
            ys, new = _gla_chunk(qw, kw, vw, la, states, gnorm, WIN, kvalid=valid[par])
            for h in range(HEADS):
                yw[h] = ys[h] if par == 0 else jnp.where(valid[1], ys[h], yw[h])
                s_ref[0, b, h] = new[h]
        for h in range(HEADS):
            y_rows[h].append(yw[h])
    for h in range(HEADS):
        y_ref[:, h * DV:(h + 1) * DV] = jnp.concatenate(y_rows[h], axis=0).astype(BF16)


def _gla_sample(p, gates, w2, b2, gnorm, s0, y_prev, s_prev, *, layer, n_seq, row0):
    nbk = SAMPLE_NB
    rb = nbk * 4
    off = row0 // rb
    rows = lambda width, col: pl.BlockSpec((rb, width), lambda i, col=col: (off + i, col))
    vec = lambda width: pl.BlockSpec((1, width), lambda i: (0, 0))
    any_spec = pl.BlockSpec(memory_space=pl.ANY)
    has_prev = s_prev is not None
    in_specs = [rows(512, C512_GQ), rows(512, C512_GK), rows(1024, C1K_GV), rows(LANES, 0),
                pl.BlockSpec((LANES, 512), lambda i: (0, 0)), vec(512), vec(D_MODEL),
                pl.BlockSpec((1, nbk, HEADS, DK, DV), lambda i: (layer, i, 0, 0, 0)),
                any_spec]
    args = [p, p, p, gates, w2, b2, gnorm, s0, y_prev]
    aliases = {8: 0}
    if has_prev:
        in_specs.append(any_spec)
        args.append(s_prev)
        aliases[9] = 1
    return pl.pallas_call(
        functools.partial(_gla_sample_body, NB=nbk, has_prev=has_prev),
        grid=(n_seq // nbk,),
        in_specs=in_specs,
        out_specs=(rows(D_MODEL, 0), pl.BlockSpec((1, nbk, HEADS, DK, DV), lambda i: (layer, i, 0, 0, 0))),
        out_shape=(jax.ShapeDtypeStruct(y_prev.shape, BF16), jax.ShapeDtypeStruct(s0.shape, F32)),
        input_output_aliases=aliases,
        compiler_params=_cparams(1, 48), name="gla_sample",
    )(*args)


def _lru_gates(xc, wa_ref, ba_ref, wi_ref, bi_ref, lam_ref):
    ra, ri = [], []
    for nb in range(LRU_BLOCKS):
        xs = xc[:, nb * LRU_BW:(nb + 1) * LRU_BW].astype(BF16)
        ra.append(_nn(xs, wa_ref[nb]))
        ri.append(_nn(xs, wi_ref[nb]))
    r = _sigmoid(jnp.concatenate(ra, axis=-1) + ba_ref[...])
    ig = _sigmoid(jnp.concatenate(ri, axis=-1) + bi_ref[...])
    log_a = (-LRU_C) * r * _softplus(-lam_ref[...])
    return jnp.exp(log_a), ig * xc, 2.0 * log_a


def _lru_prompt_body(x_ref, wc_ref, bc_ref, wa_ref, ba_ref, wi_ref, bi_ref, lam_ref,
                     h_ref, hfin_ref, cfin_ref, xbuf, hcar, *, TB):
    t = pl.program_id(1)

    @pl.when(t == 0)
    def _():
        xbuf[0:SUBLANES, :] = jnp.zeros((SUBLANES, D_MODEL), F32)
        hcar[...] = jnp.zeros_like(hcar)

    x = x_ref[...].astype(F32)
    xbuf[SUBLANES:SUBLANES + TB, :] = x
    wc = wc_ref[...]
    xc = bc_ref[...] + wc[3:4, :] * x
    for j in range(CONV_W - 1):
        xc = xc + wc[j:j + 1, :] * xbuf[SUBLANES - 3 + j:SUBLANES - 3 + j + TB, :]
    xbuf[0:SUBLANES, :] = x[TB - SUBLANES:TB, :]

    a, gx, two_log_a = _lru_gates(xc, wa_ref, ba_ref, wi_ref, bi_ref, lam_ref)
    mult = jnp.sqrt(1.0 - jnp.exp(two_log_a))
    row = lax.broadcasted_iota(jnp.int32, (TB, 1), 0)
    mult = jnp.where(row + t * TB == 0, 1.0, mult)
    u = mult * gx
    d = 1
    while d < TB:
        keep = row >= d
        a_sh = jnp.where(keep, pltpu.roll(a, d, 0), 1.0)
        u_sh = jnp.where(keep, pltpu.roll(u, d, 0), 0.0)
        u = a * u_sh + u
        a = a * a_sh
        d *= 2
    h = a * hcar[...] + u
    h_ref[...] = h.astype(BF16)
    hcar[...] = h[TB - 1:TB, :]
    hfin_ref[0] = h[TB - 1:TB, :]
    cfin_ref[0] = x[TB - (CONV_W - 1):TB, :]


def _lru_prompt(p, wc, bc, wa, ba, wi, bi, lam, *, nb, t_len, nt_rows):
    tb = MIX_TB
    nblk = t_len // tb
    vec = pl.BlockSpec((1, D_MODEL), lambda b, t: (0, 0))
    wblk = pl.BlockSpec((LRU_BLOCKS, LRU_BW, LRU_BW), lambda b, t: (0, 0, 0))
    return pl.pallas_call(
        functools.partial(_lru_prompt_body, TB=tb),
        grid=(nb, nblk),
        in_specs=[pl.BlockSpec((tb, D_MODEL), lambda b, t: (b * nblk + t, C1K_LX)),
                  pl.BlockSpec((CONV_W, D_MODEL), lambda b, t: (0, 0)), vec, wblk, vec, wblk, vec, vec],
        out_specs=(pl.BlockSpec((tb, D_MODEL), lambda b, t: (b * nblk + t, 0)),
                   pl.BlockSpec((1, 1, D_MODEL), lambda b, t: (b, 0, 0)),
                   pl.BlockSpec((1, CONV_W - 1, D_MODEL), lambda b, t: (b, 0, 0))),
        out_shape=(jax.ShapeDtypeStruct((nt_rows, D_MODEL), BF16),
                   jax.ShapeDtypeStruct((nb, 1, D_MODEL), F32),
                   jax.ShapeDtypeStruct((nb, CONV_W - 1, D_MODEL), F32)),
        scratch_shapes=[pltpu.VMEM((tb + SUBLANES, D_MODEL), F32), pltpu.VMEM((1, D_MODEL), F32)],
        compiler_params=_cparams(2, 40), name="lru_prompt",
    )(p, wc, bc, wa, ba, wi, bi, lam)


def _lru_sample_body(x_ref, buf_ref, h0_ref, wc_ref, bc_ref, wa_ref, ba_ref, wi_ref, bi_ref, lam_ref,
                     h_ref, hfin_ref, cfin_ref, *, T):
    wc = wc_ref[...]
    xs = [buf_ref[j] for j in range(CONV_W - 1)] + [x_ref[t].astype(F32) for t in range(T)]
    h = h0_ref[...]
    for t in range(T):
        xc = bc_ref[...]
        for j in range(CONV_W):
            xc = xc + wc[j:j + 1, :] * xs[t + j]
        a, gx, two_log_a = _lru_gates(xc, wa_ref, ba_ref, wi_ref, bi_ref, lam_ref)
        h = a * h + jnp.sqrt(1.0 - jnp.exp(two_log_a)) * gx
        h_ref[t] = h.astype(BF16)
    hfin_ref[...] = h
    for j in range(CONV_W - 1):
        cfin_ref[j] = xs[T + j]


def _lru_sample(x_tm, buf_tm, h0, wc, bc, wa, ba, wi, bi, lam):
    t_len, n_seq, _ = x_tm.shape
    return pl.pallas_call(
        functools.partial(_lru_sample_body, T=t_len),
        out_shape=(jax.ShapeDtypeStruct((t_len, n_seq, D_MODEL), BF16),
                   jax.ShapeDtypeStruct((n_seq, D_MODEL), F32),
                   jax.ShapeDtypeStruct((CONV_W - 1, n_seq, D_MODEL), F32)),
        compiler_params=pltpu.CompilerParams(vmem_limit_bytes=40 * 1024 * 1024), name="lru_sample",
    )(x_tm, buf_tm, h0, wc, bc, wa, ba, wi, bi, lam)


_W_IN_SPLITS = (512, 512, 1024, 1024, 8, 512, 512, 1024, 1024, 16, 1024, 1024, 3072)
_W_IN_NAMES = ("mq", "mk", "mv", "mo", "mg_if", "gq", "gk", "gv", "gg", "glr", "lx", "ly", "mgate")
_P_ORDER = ("mq", "mk", "mv", "gq", "gk", "gv", "lx", "mo", "gg", "ly", "mgate")


def _split_w_in(w):
    out, a = {}, 0
    for name, width in zip(_W_IN_NAMES, _W_IN_SPLITS):
        out[name] = w[:, a:a + width]
        a += width
    return out


def kernel(x_prompt, x_sample, state_mlstm_C, state_mlstm_n, state_mlstm_m, state_gla_S, state_lru_h, state_lru_conv,
           norm_ffn1, w_ffn1_in, w_ffn1_out, norm_mix, w_in, b_ml_if, g_ml_norm, w_gla_lr2, b_gla_gate, g_gla_norm,
           w_conv, b_conv, w_lru_a, b_lru_a, w_lru_i, b_lru_i, lru_lambda, w_out, norm_ffn2, w_ffn2_in, w_ffn2_out,
           norm_final):
    nb, t_len, d = x_prompt.shape
    n_seq, ts, _ = x_sample.shape
    depth = w_in.shape[0]
    assert d == D_MODEL and ts == WIN // 2 and t_len % MIX_TB == 0 and n_seq % SAMPLE_NB == 0
    n_prompt = nb * t_len
    n_sample = n_seq * ts
    nt_rows = n_prompt + n_sample
    assert n_prompt % (SAMPLE_NB * ts) == 0

    x = jnp.concatenate([x_prompt.reshape(n_prompt, d), x_sample.reshape(n_sample, d)], axis=0)

    new_p = {k: [] for k in ("C", "n", "m", "S", "h", "conv")}
    new_s = {k: [] for k in ("n", "m", "h", "conv")}
    s_c_all, s_s_all = None, None
    y = None
    for l in range(depth):
        cols = _split_w_in(w_in[l])
        w_big = jnp.concatenate([cols[k] for k in _P_ORDER], axis=1).astype(BF16)
        w_small = jnp.concatenate([cols["mg_if"], cols["glr"],
                                   jnp.zeros((d, LANES - 2 * HEADS - GLA_RANK), F32)], axis=1).astype(BF16)
        bias_if = jnp.concatenate([b_ml_if[l], jnp.zeros((LANES - 2 * HEADS,), F32)]).reshape(1, LANES)
        w2 = jnp.zeros((LANES, HEADS * DK), F32).at[2 * HEADS:2 * HEADS + GLA_RANK].set(w_gla_lr2[l]).astype(BF16)
        b2 = b_gla_gate[l].reshape(1, HEADS * DK)
        g_ml = g_ml_norm[l].reshape(1, d)
        g_gla = g_gla_norm[l].reshape(1, d)
        wc, bc = w_conv[l], b_conv[l].reshape(1, d)
        wa, wi = w_lru_a[l].astype(BF16), w_lru_i[l].astype(BF16)
        ba, bi, lam = b_lru_a[l].reshape(1, d), b_lru_i[l].reshape(1, d), lru_lambda[l].reshape(1, d)

        x1, xn = _ffn(x, norm_ffn1[l], w_ffn1_in[l].astype(BF16), w_ffn1_out[l].astype(BF16), norm_mix[l], final=False)
        p, gates = _proj(xn, w_big, w_small)

        y_ml, p_c, p_n, p_m = _mlstm_prompt(p, gates, bias_if, g_ml, nb=nb, t_len=t_len, nt_rows=nt_rows)
        y_ml, s_c_all, s_n, s_m = _mlstm_sample(p, gates, bias_if, g_ml, state_mlstm_C, state_mlstm_n, state_mlstm_m,
                                                y_ml, s_c_all, layer=l, n_seq=n_seq, row0=n_prompt)
        y_gla, p_s = _gla_prompt(p, gates, w2, b2, g_gla, nb=nb, t_len=t_len, nt_rows=nt_rows)
        y_gla, s_s_all = _gla_sample(p, gates, w2, b2, g_gla, state_gla_S, y_gla, s_s_all,
                                     layer=l, n_seq=n_seq, row0=n_prompt)
        h_lru, p_h, p_conv = _lru_prompt(p, wc, bc, wa, ba, wi, bi, lam, nb=nb, t_len=t_len, nt_rows=nt_rows)
        lx_s = p[n_prompt:, C1K_LX * 1024:(C1K_LX + 1) * 1024].reshape(n_seq, ts, d).transpose(1, 0, 2)
        hs_tm, s_h, s_conv_tm = _lru_sample(lx_s, state_lru_conv[l].transpose(1, 0, 2), state_lru_h[l],
                                            wc, bc, wa, ba, wi, bi, lam)
        h_lru = lax.dynamic_update_slice(h_lru, hs_tm.transpose(1, 0, 2).reshape(n_sample, d), (n_prompt, 0))

        x2 = _merge(x1, p, y_ml, y_gla, h_lru, w_out[l].astype(BF16))
        last = l == depth - 1
        g_next = norm_final if last else norm_ffn1[l + 1]
        if last:
            y = _ffn(x2, norm_ffn2[l], w_ffn2_in[l].astype(BF16), w_ffn2_out[l].astype(BF16), g_next, final=True)
        else:
            x, _ = _ffn(x2, norm_ffn2[l], w_ffn2_in[l].astype(BF16), w_ffn2_out[l].astype(BF16), g_next, final=False)

        new_p["C"].append(p_c)
        new_p["n"].append(p_n)
        new_p["m"].append(p_m[:, :, 0])
        new_p["S"].append(p_s)
        new_p["h"].append(p_h[:, 0, :])
        new_p["conv"].append(p_conv)
        new_s["n"].append(s_n)
        new_s["m"].append(s_m[:, :, 0])
        new_s["h"].append(s_h)
        new_s["conv"].append(s_conv_tm.transpose(1, 0, 2))

    y_prompt = y[:n_prompt].reshape(nb, t_len, d)
    y_sample = y[n_prompt:].reshape(n_seq, ts, d)
    st = lambda xs: jnp.stack(xs)
    return (y_prompt, y_sample,
            st(new_p["C"]), st(new_p["n"]), st(new_p["m"]), st(new_p["S"]), st(new_p["h"]), st(new_p["conv"]),
            s_c_all, st(new_s["n"]), st(new_s["m"]), s_s_all, st(new_s["h"]), st(new_s["conv"]))
```

```python
import functools

import jax
import jax.numpy as jnp
from jax import lax
from jax.experimental import pallas as pl
from jax.experimental.pallas import tpu as pltpu

F32 = jnp.float32
BF16 = jnp.bfloat16

D_MODEL = 1024
D_FF = 2816
HEADS = 4
DK = 128
DV = 256
GLA_RANK = 16
GLA_TAU = 16.0
LRU_BLOCKS = 8
LRU_BW = 128
LRU_C = 8.0
CONV_W = 4
EPS = 1e-6
NEG = -1e30

LANES = 128
SUBLANES = 8
MXU_N = 256

P_COLS = 11264
C512_MQ, C512_MK, C512_GQ, C512_GK = 0, 1, 4, 5
C1K_MV, C1K_GV, C1K_LX, C1K_MO, C1K_GG, C1K_LY, C1K_MG0 = 1, 3, 4, 5, 6, 7, 8

ML_CHUNK = 256
GLA_CHUNK = 128
MIX_TB = 256
TS = 4
TS_SHIFT = 2
SAMPLE_NB = 16
WIN = 8
FF_CHUNK = MXU_N


def _cparams(n_axes, vmem_mib):
    return pltpu.CompilerParams(dimension_semantics=("arbitrary",) * n_axes,
                                vmem_limit_bytes=vmem_mib * 1024 * 1024)


def _pick_tile(n, target, mult):
    best = None
    for t in range(mult, min(n, target) + 1, mult):
        if n % t == 0:
            best = t
    assert best is not None, (n, target, mult)
    return best


def _sigmoid(x):
    return 0.5 * jnp.tanh(0.5 * x) + 0.5


def _log_sigmoid(x):
    return jnp.minimum(x, 0.0) - jnp.log(1.0 + jnp.exp(-jnp.abs(x)))


def _softplus(x):
    return jnp.maximum(x, 0.0) + jnp.log(1.0 + jnp.exp(-jnp.abs(x)))


def _gelu_tanh(x):
    return 0.5 * x * (1.0 + jnp.tanh(0.7978845608028654 * (x + 0.044715 * (x * x * x))))


def _rms(x, g):
    return x * lax.rsqrt(jnp.mean(x * x, axis=-1, keepdims=True) + EPS) * g


def _nn(a, b):
    return jnp.dot(a, b, preferred_element_type=F32)


def _nt(a, b):
    return lax.dot_general(a, b, (((1,), (1,)), ((), ())), preferred_element_type=F32)


def _tn(a, b):
    return lax.dot_general(a, b, (((0,), (0,)), ((), ())), preferred_element_type=F32)


def _split3(x):
    hi = x.astype(BF16)
    r1 = x - hi.astype(F32)
    mid = r1.astype(BF16)
    lo = (r1 - mid.astype(F32)).astype(BF16)
    return hi, mid, lo


def _sel_left(sel, x):
    hi, mid, lo = _split3(x)
    return _nn(sel, hi) + _nn(sel, mid) + _nn(sel, lo)


def _sel_right(x, sel):
    hi, mid, lo = _split3(x)
    return _nn(hi, sel) + _nn(mid, sel) + _nn(lo, sel)


def _one_hot(mask):
    return jnp.where(mask, 1.0, 0.0).astype(BF16)


def _head_layernorm(h, g):
    mu = jnp.mean(h, axis=-1, keepdims=True)
    hc = h - mu
    return hc * lax.rsqrt(jnp.mean(hc * hc, axis=-1, keepdims=True) + EPS) * g


def _head_rmsnorm(h, g):
    return h * lax.rsqrt(jnp.mean(h * h, axis=-1, keepdims=True) + EPS) * g


def _ffn_body(x_ref, g_ref, win_ref, wout_ref, gn_ref, *rest, n_chunks, final):
    if final:
        y_ref, xn_s, acc_s = rest
    else:
        y_ref, xn_ref, xn_s, acc_s = rest
    xn_s[...] = _rms(x_ref[...], g_ref[...]).astype(BF16)
    acc_s[...] = jnp.zeros_like(acc_s)

    def chunk(c, carry):
        xn = xn_s[...]
        g = _nn(xn, win_ref[c])
        u = _nn(xn, win_ref[n_chunks + c])
        a = (g * _sigmoid(g)) * u
        acc_s[...] += _nn(a.astype(BF16), wout_ref[c])
        return carry

    lax.fori_loop(0, n_chunks, chunk, 0)
    xo = x_ref[...] + 0.5 * acc_s[...]
    if final:
        y_ref[...] = _rms(xo, gn_ref[...])
    else:
        y_ref[...] = xo
        xn_ref[...] = _rms(xo, gn_ref[...]).astype(BF16)


def _ffn(x, g, w_in_c, w_out_c, g_next, *, final):
    rows = x.shape[0]
    tm = _pick_tile(rows, 1024, 16)
    n_chunks = w_out_c.shape[0]
    resident = lambda shape: pl.BlockSpec(shape, lambda i: (0,) * len(shape), pipeline_mode=pl.Buffered(1))
    row_spec = pl.BlockSpec((tm, D_MODEL), lambda i: (i, 0))
    vec = pl.BlockSpec((1, D_MODEL), lambda i: (0, 0))
    if final:
        out_shape = jax.ShapeDtypeStruct((rows, D_MODEL), F32)
        out_specs = row_spec
    else:
        out_shape = (jax.ShapeDtypeStruct((rows, D_MODEL), F32), jax.ShapeDtypeStruct((rows, D_MODEL), BF16))
        out_specs = (row_spec, row_spec)
    return pl.pallas_call(
        functools.partial(_ffn_body, n_chunks=n_chunks, final=final),
        grid=(rows // tm,),
        in_specs=[row_spec, vec, resident(w_in_c.shape), resident(w_out_c.shape), vec],
        out_specs=out_specs, out_shape=out_shape,
        scratch_shapes=[pltpu.VMEM((tm, D_MODEL), BF16), pltpu.VMEM((tm, D_MODEL), F32)],
        compiler_params=_cparams(1, 56), name="ffn_final" if final else "ffn",
    )(x, g.reshape(1, D_MODEL), w_in_c, w_out_c, g_next.reshape(1, D_MODEL))


def _proj_body(xn_ref, w_ref, ws_ref, p_ref, g_ref):
    xn = xn_ref[...]
    p_ref[...] = _nn(xn, w_ref[...]).astype(BF16)

    @pl.when(pl.program_id(1) == 0)
    def _():
        g_ref[...] = _nn(xn, ws_ref[...])


def _proj(xn, w_big, w_small):
    rows = xn.shape[0]
    tm = _pick_tile(rows, 2048, 16)
    tn = 1024
    return pl.pallas_call(
        _proj_body, grid=(rows // tm, P_COLS // tn),
        in_specs=[pl.BlockSpec((tm, D_MODEL), lambda i, j: (i, 0)),
                  pl.BlockSpec((D_MODEL, tn), lambda i, j: (0, j)),
                  pl.BlockSpec((D_MODEL, LANES), lambda i, j: (0, 0))],
        out_specs=(pl.BlockSpec((tm, tn), lambda i, j: (i, j)),
                   pl.BlockSpec((tm, LANES), lambda i, j: (i, 0))),
        out_shape=(jax.ShapeDtypeStruct((rows, P_COLS), BF16), jax.ShapeDtypeStruct((rows, LANES), F32)),
        compiler_params=_cparams(2, 48), name="proj",
    )(xn, w_big, w_small)


def _merge_body(x_ref, mo_ref, gg_ref, ly_ref, m0_ref, m1_ref, m2_ref, yml_ref, ygla_ref, hlru_ref, w_ref, o_ref):
    f = lambda r: r[...].astype(F32)
    gg = f(gg_ref)
    y_ml = f(yml_ref) * _sigmoid(f(mo_ref))
    y_gla = f(ygla_ref) * (gg * _sigmoid(gg))
    y_lru = f(hlru_ref) * _gelu_tanh(f(ly_ref))
    merged = _sigmoid(f(m0_ref)) * y_ml + _sigmoid(f(m1_ref)) * y_gla + _sigmoid(f(m2_ref)) * y_lru
    o_ref[...] = x_ref[...] + _nn(merged.astype(BF16), w_ref[...])


def _merge(x, p, y_ml, y_gla, h_lru, w_out):
    rows = x.shape[0]
    tm = _pick_tile(rows, 512, 16)
    row = lambda c: pl.BlockSpec((tm, D_MODEL), lambda i, c=c: (i, c))
    return pl.pallas_call(
        _merge_body, grid=(rows // tm,),
        in_specs=[row(0), row(C1K_MO), row(C1K_GG), row(C1K_LY), row(C1K_MG0), row(C1K_MG0 + 1), row(C1K_MG0 + 2),
                  row(0), row(0), row(0), pl.BlockSpec((D_MODEL, D_MODEL), lambda i: (0, 0))],
        out_specs=row(0),
        out_shape=jax.ShapeDtypeStruct((rows, D_MODEL), F32),
        compiler_params=_cparams(1, 56), name="merge",
    )(x, p, p, p, p, p, p, y_ml, y_gla, h_lru, w_out)


def _mlstm_chunk(q, k, v, gi, lf, states, gnorm, L):
    rowi = lax.broadcasted_iota(jnp.int32, (L, L), 0)
    coli = lax.broadcasted_iota(jnp.int32, (L, L), 1)
    causal = rowi >= coli
    b_cols = _sel_left(_one_hot(causal), lf)
    gi_t = gi.T[0:SUBLANES]
    lf_t = lf.T[0:SUBLANES]
    b_rows = _sel_right(lf_t, _one_hot(rowi <= coli))
    scale = DK ** -0.5
    ys, new_states = [], []
    for h in range(HEADS):
        c_mat, n_vec, m_prev = states[h]
        qh = q[:, h * DK:(h + 1) * DK]
        kh = k[:, h * DK:(h + 1) * DK]
        vh = v[:, h * DV:(h + 1) * DV]
        i_col = gi[:, h:h + 1]
        b_col = b_cols[:, HEADS + h:HEADS + h + 1]
        c_row = gi_t[h:h + 1, :] - b_rows[HEADS + h:HEADS + h + 1, :]
        dm = jnp.where(causal, c_row, NEG)
        big_m = jnp.maximum(m_prev, jnp.max(dm, axis=-1, keepdims=True))
        pmat = jnp.exp(dm - big_m)
        w = (_nt(qh, kh) * scale) * pmat
        s_inter = jnp.exp(m_prev - big_m)
        num = _nn(w.astype(BF16), vh) + s_inter * _nt(qh, c_mat.astype(BF16))
        qn = jnp.sum(qh.astype(F32) * n_vec, axis=-1, keepdims=True)
        den = jnp.sum(w, axis=-1, keepdims=True) + s_inter * qn
        hh = num / jnp.maximum(jnp.abs(den), jnp.exp(-(b_col + big_m)))
        ys.append(_head_layernorm(hh, gnorm[:, h * DV:(h + 1) * DV]))
        m_last = big_m[L - 1:L, :]
        wk = jnp.exp((i_col - b_col) - m_last)
        decay = jnp.exp(m_prev - m_last)
        kw = kh.astype(F32) * (wk * scale)
        c_new = decay * c_mat + _tn(vh, kw.astype(BF16))
        n_new = decay * n_vec + jnp.sum(kw, axis=0, keepdims=True)
        m_new = b_col[L - 1:L, :] + m_last
        new_states.append((c_new, n_new, m_new))
    return ys, new_states


def _mlstm_prompt_body(q_ref, k_ref, v_ref, g_ref, bias_ref, gn_ref, y_ref, c_ref, n_ref, m_ref, *, L, NC):
    @pl.when(pl.program_id(1) == 0)
    def _():
        c_ref[...] = jnp.zeros_like(c_ref)
        n_ref[...] = jnp.zeros_like(n_ref)
        m_ref[...] = jnp.zeros_like(m_ref)

    gates = g_ref[...] + bias_ref[...]
    lf_all = _log_sigmoid(gates)
    gnorm = gn_ref[...]
    for c in range(NC):
        sl = slice(c * L, (c + 1) * L)
        states = [(c_ref[0, h], n_ref[0, h:h + 1, :], m_ref[0, h:h + 1, 0:1]) for h in range(HEADS)]
        ys, new = _mlstm_chunk(q_ref[sl, :], k_ref[sl, :], v_ref[sl, :], gates[sl], lf_all[sl], states, gnorm, L)
        for h in range(HEADS):
            y_ref[sl, h * DV:(h + 1) * DV] = ys[h].astype(BF16)
            c_ref[0, h] = new[h][0]
            n_ref[0, h:h + 1, :] = new[h][1]
            m_ref[0, h:h + 1, :] = jnp.broadcast_to(new[h][2], (1, LANES))


def _mlstm_prompt(p, gates, bias, gnorm, *, nb, t_len):
    tb = MIX_TB
    nblk = t_len // tb
    rows = lambda width, col: pl.BlockSpec((tb, width), lambda b, t, col=col: (b * nblk + t, col))
    vec = lambda width: pl.BlockSpec((1, width), lambda b, t: (0, 0))
    return pl.pallas_call(
        functools.partial(_mlstm_prompt_body, L=ML_CHUNK, NC=tb // ML_CHUNK),
        grid=(nb, nblk),
        in_specs=[rows(512, C512_MQ), rows(512, C512_MK), rows(1024, C1K_MV), rows(LANES, 0), vec(LANES), vec(D_MODEL)],
        out_specs=(rows(D_MODEL, 0),
                   pl.BlockSpec((1, HEADS, DV, DK), lambda b, t: (b, 0, 0, 0)),
                   pl.BlockSpec((1, HEADS, DK), lambda b, t: (b, 0, 0)),
                   pl.BlockSpec((1, HEADS, LANES), lambda b, t: (b, 0, 0))),
        out_shape=(jax.ShapeDtypeStruct((nb * t_len, D_MODEL), BF16),
                   jax.ShapeDtypeStruct((nb, HEADS, DV, DK), F32),
                   jax.ShapeDtypeStruct((nb, HEADS, DK), F32),
                   jax.ShapeDtypeStruct((nb, HEADS, LANES), F32)),
        compiler_params=_cparams(2, 40), name="mlstm_prompt",
    )(p, p, p, gates, bias, gnorm)


def _seq_masks(L, nseq):
    rowi = lax.broadcasted_iota(jnp.int32, (L, L), 0)
    coli = lax.broadcasted_iota(jnp.int32, (L, L), 1)
    same = (rowi >> TS_SHIFT) == (coli >> TS_SHIFT)
    sb = lax.broadcasted_iota(jnp.int32, (nseq, L), 0)
    sr = lax.broadcasted_iota(jnp.int32, (nseq, L), 1)
    first = _one_hot(sr == sb * TS)
    member = _one_hot((sr >> TS_SHIFT) == sb)
    return rowi, coli, same, first, member


def _mlstm_sample_body(q_ref, k_ref, v_ref, g_ref, bias_ref, gn_ref, c0_ref, n0_ref, nrep_ref, mrep_ref, *rest,
                       NB, has_prev):
    y_ref, c_ref, n_ref, m_ref = rest[-4:]
    L = NB * TS
    rowi, coli, same, first, member = _seq_masks(L, NB)
    causal = jnp.logical_and(same, rowi >= coli)
    gates = g_ref[...] + bias_ref[...]
    lf = _log_sigmoid(gates)
    b_cols = _sel_left(_one_hot(causal), lf)
    btot_cols = _sel_left(_one_hot(same), lf)
    gi_t = gates.T[0:SUBLANES]
    lf_t = lf.T[0:SUBLANES]
    b_rows = _sel_right(lf_t, _one_hot(jnp.logical_and(same, rowi <= coli)))
    gnorm = gn_ref[...]
    q_all = q_ref[...].astype(F32)
    k_all = k_ref[...].astype(F32)
    v_all = v_ref[...].astype(F32)
    m_rep = mrep_ref[...]
    lane = lax.broadcasted_iota(jnp.int32, (L, LANES), 1)
    upper = lax.broadcasted_iota(jnp.int32, (WIN, 1), 0) < TS
    scale = DK ** -0.5
    for h in range(HEADS):
        qf = q_all[:, h * DK:(h + 1) * DK]
        kf = k_all[:, h * DK:(h + 1) * DK]
        vf = v_all[:, h * DV:(h + 1) * DV]
        qh, kh, vh = qf.astype(BF16), kf.astype(BF16), vf.astype(BF16)
        i_col = gates[:, h:h + 1]
        b_col = b_cols[:, HEADS + h:HEADS + h + 1]
        c_row = gi_t[h:h + 1, :] - b_rows[HEADS + h:HEADS + h + 1, :]
        m_prev = m_rep[:, h:h + 1]
        dm = jnp.where(causal, c_row, NEG)
        big_m = jnp.maximum(m_prev, jnp.max(dm, axis=-1, keepdims=True))
        m_last = jnp.maximum(m_prev, jnp.max(jnp.where(same, c_row, NEG), axis=-1, keepdims=True))
        pmat = jnp.exp(dm - big_m)
        w = (_nt(qh, kh) * scale) * pmat
        s_inter = jnp.exp(m_prev - big_m)
        qc = []
        for wd in range(NB // 2):
            qw = qf[WIN * wd:WIN * (wd + 1)].astype(BF16)
            o0 = _nt(qw, c0_ref[0, 2 * wd, h].astype(BF16))
            o1 = _nt(qw, c0_ref[0, 2 * wd + 1, h].astype(BF16))
            qc.append(jnp.where(upper, o0, o1))
        num = _nn(w.astype(BF16), vh) + s_inter * jnp.concatenate(qc, axis=0)
        qn = jnp.sum(qf * nrep_ref[:, h * DK:(h + 1) * DK], axis=-1, keepdims=True)
        den = jnp.sum(w, axis=-1, keepdims=True) + s_inter * qn
        hh = num / jnp.maximum(jnp.abs(den), jnp.exp(-(b_col + big_m)))
        y_ref[:, h * DV:(h + 1) * DV] = _head_layernorm(hh, gnorm[:, h * DV:(h + 1) * DV]).astype(BF16)
        wk = jnp.exp((i_col - b_col) - m_last)
        decay_col = jnp.exp(m_prev - m_last)
        kw = kf * (wk * scale)
        m_new_col = btot_cols[:, HEADS + h:HEADS + h + 1] + m_last
        z = jnp.where(lane == 0, decay_col, jnp.where(lane == 1, m_new_col, 0.0))
        zs = _sel_left(first, z)
        n_ref[h] = zs[:, 0:1] * n0_ref[h] + _sel_left(member, kw)
        m_ref[h] = jnp.broadcast_to(zs[:, 1:2], (NB, LANES))
        for b in range(NB):
            wd, par = b // 2, b % 2
            valid = upper if par == 0 else jnp.logical_not(upper)
            vwin = jnp.where(valid, vf[WIN * wd:WIN * (wd + 1)], 0.0).astype(BF16)
            kwin = kw[WIN * wd:WIN * (wd + 1)].astype(BF16)
            c_ref[0, b, h] = zs[b:b + 1, 0:1] * c0_ref[0, b, h] + _tn(vwin, kwin)


def _mlstm_sample(p, gates, bias, gnorm, c0, n0_hm, n_rep, m_rep, c_prev, *, layer, n_seq):
    nbk = SAMPLE_NB
    rb = nbk * TS
    rows = lambda width, col: pl.BlockSpec((rb, width), lambda i, col=col: (i, col))
    vec = lambda width: pl.BlockSpec((1, width), lambda i: (0, 0))
    hm = pl.BlockSpec((HEADS, nbk, LANES), lambda i: (0, i, 0))
    has_prev = c_prev is not None
    in_specs = [rows(512, C512_MQ), rows(512, C512_MK), rows(1024, C1K_MV), rows(LANES, 0), vec(LANES), vec(D_MODEL),
                pl.BlockSpec((1, nbk, HEADS, DV, DK), lambda i: (layer, i, 0, 0, 0)),
                hm, rows(HEADS * DK, 0), rows(LANES, 0)]
    args = [p, p, p, gates, bias, gnorm, c0, n0_hm, n_rep, m_rep]
    aliases = {}
    if has_prev:
        in_specs.append(pl.BlockSpec(memory_space=pl.ANY))
        args.append(c_prev)
        aliases[len(args) - 1] = 1
    return pl.pallas_call(
        functools.partial(_mlstm_sample_body, NB=nbk, has_prev=has_prev),
        grid=(n_seq // nbk,),
        in_specs=in_specs,
        out_specs=(rows(D_MODEL, 0),
                   pl.BlockSpec((1, nbk, HEADS, DV, DK), lambda i: (layer, i, 0, 0, 0)),
                   hm, hm),
        out_shape=(jax.ShapeDtypeStruct((n_seq * TS, D_MODEL), BF16),
                   jax.ShapeDtypeStruct(c0.shape, F32),
                   jax.ShapeDtypeStruct((HEADS, n_seq, LANES), F32),
                   jax.ShapeDtypeStruct((HEADS, n_seq, LANES), F32)),
        input_output_aliases=aliases,
        compiler_params=_cparams(1, 56), name="mlstm_sample",
    )(*args)


def _ref_rows(b, m, L):
    if m >= SUBLANES:
        nb = L // m
        b3 = b.reshape(nb, m, LANES)
        r = b3[:, m // 2 - 1:m // 2, :]
        return jnp.broadcast_to(r, (nb, m, LANES)).reshape(L, LANES)
    bt = b.reshape(L // SUBLANES, SUBLANES, LANES)
    sub = lax.broadcasted_iota(jnp.int32, bt.shape, 1)
    out = None
    for j in range(SUBLANES // m):
        idx = j * m + m // 2 - 1
        rj = jnp.broadcast_to(bt[:, idx:idx + 1, :], bt.shape)
        out = rj if out is None else jnp.where(sub >= j * m, rj, out)
    return out.reshape(L, LANES)


def _gla_scores(qf, kf, b, L, top, xor_rc, row1):
    a_mat = jnp.zeros((L, L), F32)
    m = top
    while m >= 2:
        ref = _ref_rows(b, m, L)
        second = (row1 & (m // 2)) != 0
        f = jnp.exp(jnp.where(second, b - ref, ref - b))
        qt = jnp.where(second, qf * f, 0.0).astype(BF16)
        kt = jnp.where(second, 0.0, kf * f).astype(BF16)
        am = _nt(qt, kt)
        if m < L:
            am = jnp.where(xor_rc < m, am, 0.0)
        a_mat = a_mat + am
        m //= 2
    return a_mat


def _gla_chunk(q, k, v, la, states, gnorm, L):
    rowi = lax.broadcasted_iota(jnp.int32, (L, L), 0)
    coli = lax.broadcasted_iota(jnp.int32, (L, L), 1)
    tril = _one_hot(rowi >= coli)
    xor_rc = rowi ^ coli
    row1 = lax.broadcasted_iota(jnp.int32, (L, 1), 0)
    scale = DK ** -0.5
    ys, new_states = [], []
    for h in range(HEADS):
        s_mat = states[h]
        qf = q[:, h * DK:(h + 1) * DK].astype(F32) * scale
        kf = k[:, h * DK:(h + 1) * DK].astype(F32)
        vh = v[:, h * DV:(h + 1) * DV]
        b = _sel_left(tril, la[:, h * DK:(h + 1) * DK])
        a_mat = _gla_scores(qf, kf, b, L, L, xor_rc, row1)
        diag = jnp.sum(qf * kf, axis=-1, keepdims=True)
        o = (_nn(a_mat.astype(BF16), vh) + diag * vh.astype(F32)
             + _nn((qf * jnp.exp(b)).astype(BF16), s_mat.astype(BF16)))
        ys.append(_head_rmsnorm(o, gnorm[:, h * DV:(h + 1) * DV]))
        b_last = b[L - 1:L, :]
        kd = (kf * jnp.exp(b_last - b)).astype(BF16)
        e_col = jnp.broadcast_to(jnp.exp(b_last), (SUBLANES, LANES)).T[:, 0:1]
        new_states.append(e_col * s_mat + _tn(kd, vh))
    return ys, new_states


def _gla_log_decay(gates, w2_ref, b2_ref):
    return _log_sigmoid(_nn(gates.astype(BF16), w2_ref[...]) + b2_ref[...]) * (1.0 / GLA_TAU)


def _gla_prompt_body(q_ref, k_ref, v_ref, g_ref, w2_ref, b2_ref, gn_ref, y_ref, s_ref, *, L, NC):
    @pl.when(pl.program_id(1) == 0)
    def _():
        s_ref[...] = jnp.zeros_like(s_ref)

    la_all = _gla_log_decay(g_ref[...], w2_ref, b2_ref)
    gnorm = gn_ref[...]
    for c in range(NC):
        sl = slice(c * L, (c + 1) * L)
        states = [s_ref[0, h] for h in range(HEADS)]
        ys, new = _gla_chunk(q_ref[sl, :], k_ref[sl, :], v_ref[sl, :], la_all[sl], states, gnorm, L)
        for h in range(HEADS):
            y_ref[sl, h * DV:(h + 1) * DV] = ys[h].astype(BF16)
            s_ref[0, h] = new[h]


def _gla_prompt(p, gates, w2, b2, gnorm, *, nb, t_len):
    tb = MIX_TB
    nblk = t_len // tb
    rows = lambda width, col: pl.BlockSpec((tb, width), lambda b, t, col=col: (b * nblk + t, col))
    vec = lambda width: pl.BlockSpec((1, width), lambda b, t: (0, 0))
    return pl.pallas_call(
        functools.partial(_gla_prompt_body, L=GLA_CHUNK, NC=tb // GLA_CHUNK),
        grid=(nb, nblk),
        in_specs=[rows(512, C512_GQ), rows(512, C512_GK), rows(1024, C1K_GV), rows(LANES, 0),
                  pl.BlockSpec((LANES, 512), lambda b, t: (0, 0)), vec(512), vec(D_MODEL)],
        out_specs=(rows(D_MODEL, 0), pl.BlockSpec((1, HEADS, DK, DV), lambda b, t: (b, 0, 0, 0))),
        out_shape=(jax.ShapeDtypeStruct((nb * t_len, D_MODEL), BF16), jax.ShapeDtypeStruct((nb, HEADS, DK, DV), F32)),
        compiler_params=_cparams(2, 40), name="gla_prompt",
    )(p, p, p, gates, w2, b2, gnorm)


def _gla_sample_body(q_ref, k_ref, v_ref, g_ref, w2_ref, b2_ref, gn_ref, s0_ref, *rest, NB, has_prev):
    y_ref, s_ref = rest[-2:]
    L = NB * TS
    rowi, coli, same, first, _ = _seq_masks(L, NB)
    tril_seq = _one_hot(jnp.logical_and(same, rowi >= coli))
    full_seq = _one_hot(same)
    xor_rc = rowi ^ coli
    row1 = lax.broadcasted_iota(jnp.int32, (L, 1), 0)
    la_all = _gla_log_decay(g_ref[...], w2_ref, b2_ref)
    gnorm = gn_ref[...]
    q_all = q_ref[...].astype(F32)
    k_all = k_ref[...].astype(F32)
    v_all = v_ref[...].astype(F32)
    upper = lax.broadcasted_iota(jnp.int32, (WIN, 1), 0) < TS
    scale = DK ** -0.5
    for h in range(HEADS):
        qf = q_all[:, h * DK:(h + 1) * DK] * scale
        kf = k_all[:, h * DK:(h + 1) * DK]
        vf = v_all[:, h * DV:(h + 1) * DV]
        vh = vf.astype(BF16)
        la = la_all[:, h * DK:(h + 1) * DK]
        b = _sel_left(tril_seq, la)
        btot = _sel_left(full_seq, la)
        a_mat = _gla_scores(qf, kf, b, L, TS, xor_rc, row1)
        diag = jnp.sum(qf * kf, axis=-1, keepdims=True)
        qe = qf * jnp.exp(b)
        inter = []
        for wd in range(NB // 2):
            qw = qe[WIN * wd:WIN * (wd + 1)].astype(BF16)
            o0 = _nn(qw, s0_ref[0, 2 * wd, h].astype(BF16))
            o1 = _nn(qw, s0_ref[0, 2 * wd + 1, h].astype(BF16))
            inter.append(jnp.where(upper, o0, o1))
        o = _nn(a_mat.astype(BF16), vh) + diag * vf + jnp.concatenate(inter, axis=0)
        y_ref[:, h * DV:(h + 1) * DV] = _head_rmsnorm(o, gnorm[:, h * DV:(h + 1) * DV]).astype(BF16)
        kd = kf * jnp.exp(btot - b)
        e_cols = jnp.exp(_sel_left(first, btot)).T
        for bq in range(NB):
            wd, par = bq // 2, bq % 2
            valid = upper if par == 0 else jnp.logical_not(upper)
            kwin = jnp.where(valid, kd[WIN * wd:WIN * (wd + 1)], 0.0).astype(BF16)
            vwin = vf[WIN * wd:WIN * (wd + 1)].astype(BF16)
            s_ref[0, bq, h] = e_cols[:, bq:bq + 1] * s0_ref[0, bq, h] + _tn(kwin, vwin)


def _gla_sample(p, gates, w2, b2, gnorm, s0, s_prev, *, layer, n_seq):
    nbk = SAMPLE_NB
    rb = nbk * TS
    rows = lambda width, col: pl.BlockSpec((rb, width), lambda i, col=col: (i, col))
    vec = lambda width: pl.BlockSpec((1, width), lambda i: (0, 0))
    has_prev = s_prev is not None
    in_specs = [rows(512, C512_GQ), rows(512, C512_GK), rows(1024, C1K_GV), rows(LANES, 0),
                pl.BlockSpec((LANES, 512), lambda i: (0, 0)), vec(512), vec(D_MODEL),
                pl.BlockSpec((1, nbk, HEADS, DK, DV), lambda i: (layer, i, 0, 0, 0))]
    args = [p, p, p, gates, w2, b2, gnorm, s0]
    aliases = {}
    if has_prev:
        in_specs.append(pl.BlockSpec(memory_space=pl.ANY))
        args.append(s_prev)
        aliases[len(args) - 1] = 1
    return pl.pallas_call(
        functools.partial(_gla_sample_body, NB=nbk, has_prev=has_prev),
        grid=(n_seq // nbk,),
        in_specs=in_specs,
        out_specs=(rows(D_MODEL, 0), pl.BlockSpec((1, nbk, HEADS, DK, DV), lambda i: (layer, i, 0, 0, 0))),
        out_shape=(jax.ShapeDtypeStruct((n_seq * TS, D_MODEL), BF16), jax.ShapeDtypeStruct(s0.shape, F32)),
        input_output_aliases=aliases,
        compiler_params=_cparams(1, 56), name="gla_sample",
    )(*args)


def _lru_gates(xc, wa_ref, ba_ref, wi_ref, bi_ref, lam_ref):
    ra, ri = [], []
    for nb in range(LRU_BLOCKS):
        xs = xc[:, nb * LRU_BW:(nb + 1) * LRU_BW].astype(BF16)
        ra.append(_nn(xs, wa_ref[nb]))
        ri.append(_nn(xs, wi_ref[nb]))
    r = _sigmoid(jnp.concatenate(ra, axis=-1) + ba_ref[...])
    ig = _sigmoid(jnp.concatenate(ri, axis=-1) + bi_ref[...])
    a = jnp.exp(((-LRU_C) * _softplus(-lam_ref[...])) * r)
    om = 1.0 - a * a
    mult = om * lax.rsqrt(jnp.maximum(om, 1e-30))
    return a, ig * xc, mult


def _conv_shift_matrices(tb):
    nsh = CONV_W - 1
    j = jnp.arange(nsh)[:, None, None] + 1
    t = jnp.arange(tb)[None, :, None]
    s = jnp.arange(tb)[None, None, :]
    shift = (t - s == j).astype(BF16).reshape(nsh * tb, tb)
    t8 = jnp.arange(SUBLANES)[None, :, None]
    s8 = jnp.arange(SUBLANES)[None, None, :]
    tail = (s8 - t8 == SUBLANES - j).astype(BF16).reshape(nsh * SUBLANES, SUBLANES)
    return shift, tail


def _lru_prompt_body(x_ref, shift_ref, tail_ref, wc_ref, bc_ref, wa_ref, ba_ref, wi_ref, bi_ref, lam_ref,
                     h_ref, hfin_ref, cfin_ref, xprev, hcar, *, TB):
    t = pl.program_id(1)

    @pl.when(t == 0)
    def _():
        xprev[...] = jnp.zeros_like(xprev)
        hcar[...] = jnp.zeros_like(hcar)

    xb = x_ref[...]
    x = xb.astype(F32)
    nsh = CONV_W - 1
    shifted = _nn(shift_ref[...], xb)
    from_prev = _nn(tail_ref[...], xprev[...].astype(BF16))
    wc = wc_ref[...]
    xc = bc_ref[...] + wc[CONV_W - 1:CONV_W, :] * x
    for j in range(nsh):
        sj = shifted[j * TB:(j + 1) * TB]
        sj = jnp.concatenate([sj[0:SUBLANES] + from_prev[j * SUBLANES:(j + 1) * SUBLANES], sj[SUBLANES:]], axis=0)
        xc = xc + wc[CONV_W - 2 - j:CONV_W - 1 - j, :] * sj
    xprev[...] = x[TB - SUBLANES:TB, :]

    a, gx, mult = _lru_gates(xc, wa_ref, ba_ref, wi_ref, bi_ref, lam_ref)
    row8 = lax.broadcasted_iota(jnp.int32, (SUBLANES, 1), 0)
    m_first = jnp.where(row8 + t == 0, 1.0, mult[0:SUBLANES])
    u = jnp.concatenate([m_first, mult[SUBLANES:]], axis=0) * gx
    nt8 = TB // SUBLANES
    a3 = a.reshape(nt8, SUBLANES, D_MODEL)
    u3 = u.reshape(nt8, SUBLANES, D_MODEL)
    sub = lax.broadcasted_iota(jnp.int32, (1, SUBLANES, D_MODEL), 1)
    d = 1
    while d < SUBLANES:
        keep = sub >= d
        a_sh = jnp.where(keep, pltpu.roll(a3, d, 1), 1.0)
        u_sh = jnp.where(keep, pltpu.roll(u3, d, 1), 0.0)
        u3 = a3 * u_sh + u3
        a3 = a3 * a_sh
        d *= 2
    h_in = hcar[...]
    hs = []
    for j in range(nt8):
        hj = a3[j] * h_in + u3[j]
        hs.append(hj)
        h_in = hj[SUBLANES - 1:SUBLANES]
    h_ref[...] = jnp.concatenate(hs, axis=0).astype(BF16)
    hcar[...] = h_in
    hfin_ref[0] = h_in
    cfin_ref[0] = x[TB - (CONV_W - 1):TB, :]


def _lru_prompt(p, wc, bc, wa, ba, wi, bi, lam, *, nb, t_len):
    tb = MIX_TB
    nblk = t_len // tb
    vec = pl.BlockSpec((1, D_MODEL), lambda b, t: (0, 0))
    wblk = pl.BlockSpec((LRU_BLOCKS, LRU_BW, LRU_BW), lambda b, t: (0, 0, 0))
    shift, tail = _conv_shift_matrices(tb)
    whole = lambda arr: pl.BlockSpec(arr.shape, lambda b, t: (0, 0))
    return pl.pallas_call(
        functools.partial(_lru_prompt_body, TB=tb),
        grid=(nb, nblk),
        in_specs=[pl.BlockSpec((tb, D_MODEL), lambda b, t: (b * nblk + t, C1K_LX)), whole(shift), whole(tail),
                  pl.BlockSpec((CONV_W, D_MODEL), lambda b, t: (0, 0)), vec, wblk, vec, wblk, vec, vec],
        out_specs=(pl.BlockSpec((tb, D_MODEL), lambda b, t: (b * nblk + t, 0)),
                   pl.BlockSpec((1, 1, D_MODEL), lambda b, t: (b, 0, 0)),
                   pl.BlockSpec((1, CONV_W - 1, D_MODEL), lambda b, t: (b, 0, 0))),
        out_shape=(jax.ShapeDtypeStruct((nb * t_len, D_MODEL), BF16),
                   jax.ShapeDtypeStruct((nb, 1, D_MODEL), F32),
                   jax.ShapeDtypeStruct((nb, CONV_W - 1, D_MODEL), F32)),
        scratch_shapes=[pltpu.VMEM((SUBLANES, D_MODEL), F32), pltpu.VMEM((1, D_MODEL), F32)],
        compiler_params=_cparams(2, 40), name="lru_prompt",
    )(p, shift, tail, wc, bc, wa, ba, wi, bi, lam)


def _lru_sample_body(x_ref, buf_ref, h0_ref, wc_ref, bc_ref, wa_ref, ba_ref, wi_ref, bi_ref, lam_ref,
                     h_ref, hfin_ref, cfin_ref, *, T):
    wc = wc_ref[...]
    xs = [buf_ref[j] for j in range(CONV_W - 1)] + [x_ref[t].astype(F32) for t in range(T)]
    h = h0_ref[...]
    for t in range(T):
        xc = bc_ref[...]
        for j in range(CONV_W):
            xc = xc + wc[j:j + 1, :] * xs[t + j]
        a, gx, mult = _lru_gates(xc, wa_ref, ba_ref, wi_ref, bi_ref, lam_ref)
        h = a * h + mult * gx
        h_ref[t] = h.astype(BF16)
    hfin_ref[...] = h
    for j in range(CONV_W - 1):
        cfin_ref[j] = xs[T + j]


def _lru_sample(x_tm, buf_tm, h0, wc, bc, wa, ba, wi, bi, lam):
    t_len, n_seq, _ = x_tm.shape
    return pl.pallas_call(
        functools.partial(_lru_sample_body, T=t_len),
        out_shape=(jax.ShapeDtypeStruct((t_len, n_seq, D_MODEL), BF16),
                   jax.ShapeDtypeStruct((n_seq, D_MODEL), F32),
                   jax.ShapeDtypeStruct((CONV_W - 1, n_seq, D_MODEL), F32)),
        compiler_params=pltpu.CompilerParams(vmem_limit_bytes=40 * 1024 * 1024), name="lru_sample",
    )(x_tm, buf_tm, h0, wc, bc, wa, ba, wi, bi, lam)


_W_IN_SPLITS = (512, 512, 1024, 1024, 8, 512, 512, 1024, 1024, 16, 1024, 1024, 3072)
_W_IN_NAMES = ("mq", "mk", "mv", "mo", "mg_if", "gq", "gk", "gv", "gg", "glr", "lx", "ly", "mgate")
_P_ORDER = ("mq", "mk", "mv", "gq", "gk", "gv", "lx", "mo", "gg", "ly", "mgate")


def _split_w_in(w):
    out, a = {}, 0
    for name, width in zip(_W_IN_NAMES, _W_IN_SPLITS):
        out[name] = w[:, a:a + width]
        a += width
    return out


def _ffn_weights(w_in, w_out):
    n = D_FF // FF_CHUNK
    w_in_c = w_in.astype(BF16).reshape(D_MODEL, 2 * n, FF_CHUNK).transpose(1, 0, 2)
    return w_in_c, w_out.astype(BF16).reshape(n, FF_CHUNK, D_MODEL)


def kernel(x_prompt, x_sample, state_mlstm_C, state_mlstm_n, state_mlstm_m, state_gla_S, state_lru_h, state_lru_conv,
           norm_ffn1, w_ffn1_in, w_ffn1_out, norm_mix, w_in, b_ml_if, g_ml_norm, w_gla_lr2, b_gla_gate, g_gla_norm,
           w_conv, b_conv, w_lru_a, b_lru_a, w_lru_i, b_lru_i, lru_lambda, w_out, norm_ffn2, w_ffn2_in, w_ffn2_out,
           norm_final):
    nb, t_len, d = x_prompt.shape
    n_seq, ts, _ = x_sample.shape
    depth = w_in.shape[0]
    assert d == D_MODEL and ts == TS and t_len % MIX_TB == 0 and n_seq % SAMPLE_NB == 0

    xp = x_prompt.reshape(nb * t_len, d)
    xs = x_sample.reshape(n_seq * ts, d)

    new_p = {k: [] for k in ("C", "n", "m", "S", "h", "conv")}
    new_s = {k: [] for k in ("n", "m", "h", "conv")}
    s_c_all, s_s_all = None, None
    yp = ys = None
    for l in range(depth):
        cols = _split_w_in(w_in[l])
        w_big = jnp.concatenate([cols[k] for k in _P_ORDER], axis=1).astype(BF16)
        w_small = jnp.concatenate([cols["mg_if"], cols["glr"],
                                   jnp.zeros((d, LANES - 2 * HEADS - GLA_RANK), F32)], axis=1).astype(BF16)
        bias_if = jnp.concatenate([b_ml_if[l], jnp.zeros((LANES - 2 * HEADS,), F32)]).reshape(1, LANES)
        w2 = jnp.zeros((LANES, HEADS * DK), F32).at[2 * HEADS:2 * HEADS + GLA_RANK].set(w_gla_lr2[l]).astype(BF16)
        b2 = b_gla_gate[l].reshape(1, HEADS * DK)
        g_ml = g_ml_norm[l].reshape(1, d)
        g_gla = g_gla_norm[l].reshape(1, d)
        wc, bc = w_conv[l], b_conv[l].reshape(1, d)
        wa, wi = w_lru_a[l].astype(BF16), w_lru_i[l].astype(BF16)
        ba, bi, lam = b_lru_a[l].reshape(1, d), b_lru_i[l].reshape(1, d), lru_lambda[l].reshape(1, d)
        f1_in, f1_out = _ffn_weights(w_ffn1_in[l], w_ffn1_out[l])
        f2_in, f2_out = _ffn_weights(w_ffn2_in[l], w_ffn2_out[l])
        w_o = w_out[l].astype(BF16)
        last = l == depth - 1
        g_next = norm_final if last else norm_ffn1[l + 1]

        x1, xn = _ffn(xp, norm_ffn1[l], f1_in, f1_out, norm_mix[l], final=False)
        p, gates = _proj(xn, w_big, w_small)
        y_ml, p_c, p_n, p_m = _mlstm_prompt(p, gates, bias_if, g_ml, nb=nb, t_len=t_len)
        y_gla, p_s = _gla_prompt(p, gates, w2, b2, g_gla, nb=nb, t_len=t_len)
        h_lru, p_h, p_conv = _lru_prompt(p, wc, bc, wa, ba, wi, bi, lam, nb=nb, t_len=t_len)
        x2 = _merge(x1, p, y_ml, y_gla, h_lru, w_o)
        if last:
            yp = _ffn(x2, norm_ffn2[l], f2_in, f2_out, g_next, final=True)
        else:
            xp, _ = _ffn(x2, norm_ffn2[l], f2_in, f2_out, g_next, final=False)

        x1, xn = _ffn(xs, norm_ffn1[l], f1_in, f1_out, norm_mix[l], final=False)
        p, gates = _proj(xn, w_big, w_small)
        n0_hm = state_mlstm_n[l].transpose(1, 0, 2)
        n_rep = jnp.repeat(state_mlstm_n[l].reshape(n_seq, HEADS * DK), ts, axis=0)
        m_rep = jnp.repeat(jnp.pad(state_mlstm_m[l], ((0, 0), (0, LANES - HEADS))), ts, axis=0)
        y_ml, s_c_all, s_n, s_m = _mlstm_sample(p, gates, bias_if, g_ml, state_mlstm_C, n0_hm, n_rep, m_rep,
                                                s_c_all, layer=l, n_seq=n_seq)
        y_gla, s_s_all = _gla_sample(p, gates, w2, b2, g_gla, state_gla_S, s_s_all, layer=l, n_seq=n_seq)
        lx_s = p[:, C1K_LX * 1024:(C1K_LX + 1) * 1024].reshape(n_seq, ts, d).transpose(1, 0, 2)
        hs_tm, s_h, s_conv_tm = _lru_sample(lx_s, state_lru_conv[l].transpose(1, 0, 2), state_lru_h[l],
                                            wc, bc, wa, ba, wi, bi, lam)
        h_lru = hs_tm.transpose(1, 0, 2).reshape(n_seq * ts, d)
        x2 = _merge(x1, p, y_ml, y_gla, h_lru, w_o)
        if last:
            ys = _ffn(x2, norm_ffn2[l], f2_in, f2_out, g_next, final=True)
        else:
            xs, _ = _ffn(x2, norm_ffn2[l], f2_in, f2_out, g_next, final=False)

        new_p["C"].append(p_c)
        new_p["n"].append(p_n)
        new_p["m"].append(p_m[:, :, 0])
        new_p["S"].append(p_s)
        new_p["h"].append(p_h[:, 0, :])
        new_p["conv"].append(p_conv)
        new_s["n"].append(s_n.transpose(1, 0, 2))
        new_s["m"].append(s_m[:, :, 0].T)
        new_s["h"].append(s_h)
        new_s["conv"].append(s_conv_tm.transpose(1, 0, 2))

    st = lambda arrs: jnp.stack(arrs)
    return (yp.reshape(nb, t_len, d), ys.reshape(n_seq, ts, d),
            st(new_p["C"]), st(new_p["n"]), st(new_p["m"]), st(new_p["S"]), st(new_p["h"]), st(new_p["conv"]),
            s_c_all, st(new_s["n"]), st(new_s["m"]), s_s_all, st(new_s["h"]), st(new_s["conv"]))
```

```python
import functools

import jax
import jax.numpy as jnp
from jax import lax
from jax.experimental import pallas as pl
from jax.experimental.pallas import tpu as pltpu

F32 = jnp.float32
BF16 = jnp.bfloat16

D_MODEL = 1024
D_FF = 2816
HEADS = 4
DK = 128
DV = 256
GLA_RANK = 16
GLA_TAU = 16.0
LRU_BLOCKS = 8
LRU_BW = 128
LRU_C = 8.0
CONV_W = 4
EPS = 1e-6
NEG = -1e30

LANES = 128
SUBLANES = 8
MXU_N = 256

P_COLS = 11264
C512_MQ, C512_MK, C512_GQ, C512_GK = 0, 1, 6, 7
C1K_MV, C1K_MO, C1K_GV, C1K_GG, C1K_LX, C1K_LY, C1K_MG0 = 1, 2, 4, 5, 6, 7, 8
W_IN_A = (0, 3072)
W_IN_IF = (3072, 3080)
W_IN_B = (3080, 6152)
W_IN_LR = (6152, 6168)
W_IN_C = (6168, 11288)

ML_CHUNK = 256
GLA_CHUNK = 128
MIX_TB = 512
LRU_TB = 256
TS = 4
TS_SHIFT = 2
SAMPLE_NB = 16
WIN = 8
FF_CHUNK = MXU_N


def _cparams(n_axes, vmem_mib):
    return pltpu.CompilerParams(dimension_semantics=("arbitrary",) * n_axes,
                                vmem_limit_bytes=vmem_mib * 1024 * 1024)


def _pick_tile(n, target, mult):
    best = None
    for t in range(mult, min(n, target) + 1, mult):
        if n % t == 0:
            best = t
    assert best is not None, (n, target, mult)
    return best


def _sigmoid(x):
    return 0.5 * jnp.tanh(0.5 * x) + 0.5


def _log_sigmoid(x):
    return jnp.minimum(x, 0.0) - jnp.log(1.0 + jnp.exp(-jnp.abs(x)))


def _softplus(x):
    return jnp.maximum(x, 0.0) + jnp.log(1.0 + jnp.exp(-jnp.abs(x)))


def _rms(x, g):
    return x * lax.rsqrt(jnp.mean(x * x, axis=-1, keepdims=True) + EPS) * g


def _nn(a, b):
    return jnp.dot(a, b, preferred_element_type=F32)


def _nt(a, b):
    return lax.dot_general(a, b, (((1,), (1,)), ((), ())), preferred_element_type=F32)


def _tn(a, b):
    return lax.dot_general(a, b, (((0,), (0,)), ((), ())), preferred_element_type=F32)


def _split3(x):
    hi = x.astype(BF16)
    r1 = x - hi.astype(F32)
    mid = r1.astype(BF16)
    lo = (r1 - mid.astype(F32)).astype(BF16)
    return hi, mid, lo


def _sel_left(sel, x):
    hi, mid, lo = _split3(x)
    return _nn(sel, hi) + _nn(sel, mid) + _nn(sel, lo)


def _sel_right(x, sel):
    hi, mid, lo = _split3(x)
    return _nn(hi, sel) + _nn(mid, sel) + _nn(lo, sel)


def _one_hot(mask):
    return jnp.where(mask, 1.0, 0.0).astype(BF16)


def _head_layernorm(h, g):
    mu = jnp.mean(h, axis=-1, keepdims=True)
    hc = h - mu
    return hc * lax.rsqrt(jnp.mean(hc * hc, axis=-1, keepdims=True) + EPS) * g


def _head_rmsnorm(h, g):
    return h * lax.rsqrt(jnp.mean(h * h, axis=-1, keepdims=True) + EPS) * g


def _ffn_body(x_ref, g_ref, win_ref, wout_ref, gn_ref, *rest, n_chunks, final):
    if final:
        y_ref, xn_s, acc_s = rest
    else:
        y_ref, xn_ref, xn_s, acc_s = rest
    xn_s[...] = _rms(x_ref[...], g_ref[...]).astype(BF16)
    for c in range(n_chunks):
        xn = xn_s[...]
        g = _nn(xn, win_ref[:, c * FF_CHUNK:(c + 1) * FF_CHUNK])
        u = _nn(xn, win_ref[:, D_FF + c * FF_CHUNK:D_FF + (c + 1) * FF_CHUNK])
        t = jnp.tanh(0.5 * g)
        a = (g * (t + 1.0)) * u
        part = _nn(a.astype(BF16), wout_ref[c * FF_CHUNK:(c + 1) * FF_CHUNK, :])
        if c == 0:
            acc_s[...] = part
        else:
            acc_s[...] += part
    xo = x_ref[...] + 0.25 * acc_s[...]
    if final:
        y_ref[...] = _rms(xo, gn_ref[...])
    else:
        y_ref[...] = xo
        xn_ref[...] = _rms(xo, gn_ref[...]).astype(BF16)


def _ffn(x, g, w_in_c, w_out_c, g_next, *, final):
    rows = x.shape[0]
    tm = _pick_tile(rows, 1024, 16)
    n_chunks = D_FF // FF_CHUNK
    resident = lambda shape: pl.BlockSpec(shape, lambda i: (0,) * len(shape), pipeline_mode=pl.Buffered(1))
    row_spec = pl.BlockSpec((tm, D_MODEL), lambda i: (i, 0))
    vec = pl.BlockSpec((1, D_MODEL), lambda i: (0, 0))
    if final:
        out_shape = jax.ShapeDtypeStruct((rows, D_MODEL), F32)
        out_specs = row_spec
    else:
        out_shape = (jax.ShapeDtypeStruct((rows, D_MODEL), F32), jax.ShapeDtypeStruct((rows, D_MODEL), BF16))
        out_specs = (row_spec, row_spec)
    return pl.pallas_call(
        functools.partial(_ffn_body, n_chunks=n_chunks, final=final),
        grid=(rows // tm,),
        in_specs=[row_spec, vec, resident(w_in_c.shape), resident(w_out_c.shape), vec],
        out_specs=out_specs, out_shape=out_shape,
        scratch_shapes=[pltpu.VMEM((tm, D_MODEL), BF16), pltpu.VMEM((tm, D_MODEL), F32)],
        compiler_params=_cparams(1, 56), name="ffn_final" if final else "ffn",
    )(x, g.reshape(1, D_MODEL), w_in_c, w_out_c, g_next.reshape(1, D_MODEL))


def _proj_body(xn_ref, w_ref, ws_ref, p_ref, g_ref):
    xn = xn_ref[...]
    p_ref[...] = _nn(xn, w_ref[...]).astype(BF16)

    @pl.when(pl.program_id(1) == 0)
    def _():
        g_ref[...] = _nn(xn, ws_ref[...])


def _proj(xn, w_big, w_small):
    rows = xn.shape[0]
    tm = _pick_tile(rows, 2048, 16)
    tn = 1024
    return pl.pallas_call(
        _proj_body, grid=(rows // tm, P_COLS // tn),
        in_specs=[pl.BlockSpec((tm, D_MODEL), lambda i, j: (i, 0)),
                  pl.BlockSpec((D_MODEL, tn), lambda i, j: (0, j)),
                  pl.BlockSpec((D_MODEL, LANES), lambda i, j: (0, 0))],
        out_specs=(pl.BlockSpec((tm, tn), lambda i, j: (i, j)),
                   pl.BlockSpec((tm, LANES), lambda i, j: (i, 0))),
        out_shape=(jax.ShapeDtypeStruct((rows, P_COLS), BF16), jax.ShapeDtypeStruct((rows, LANES), F32)),
        compiler_params=_cparams(2, 48), name="proj",
    )(xn, w_big, w_small)


def _merge_body(x_ref, mo_ref, gg_ref, ly_ref, m0_ref, m1_ref, m2_ref, yml_ref, ygla_ref, hlru_ref, w_ref, o_ref):
    f = lambda r: r[...].astype(F32)
    th = lambda r: jnp.tanh(0.5 * f(r)) + 1.0
    gg = f(gg_ref)
    ly = f(ly_ref)
    y_ml = f(yml_ref) * th(mo_ref)
    y_gla = f(ygla_ref) * (gg * th(gg_ref))
    y_lru = f(hlru_ref) * (ly * (1.0 + jnp.tanh(0.7978845608028654 * (ly + 0.044715 * (ly * ly * ly)))))
    merged4 = th(m0_ref) * y_ml + th(m1_ref) * y_gla + th(m2_ref) * y_lru
    o_ref[...] = x_ref[...] + _nn((0.25 * merged4).astype(BF16), w_ref[...])


def _merge(x, p, y_ml, y_gla, h_lru, w_out):
    rows = x.shape[0]
    tm = _pick_tile(rows, 512, 16)
    row = lambda c: pl.BlockSpec((tm, D_MODEL), lambda i, c=c: (i, c))
    return pl.pallas_call(
        _merge_body, grid=(rows // tm,),
        in_specs=[row(0), row(C1K_MO), row(C1K_GG), row(C1K_LY), row(C1K_MG0), row(C1K_MG0 + 1), row(C1K_MG0 + 2),
                  row(0), row(0), row(0), pl.BlockSpec((D_MODEL, D_MODEL), lambda i: (0, 0))],
        out_specs=row(0),
        out_shape=jax.ShapeDtypeStruct((rows, D_MODEL), F32),
        compiler_params=_cparams(1, 56), name="merge",
    )(x, p, p, p, p, p, p, y_ml, y_gla, h_lru, w_out)


def _mlstm_chunk(q, k, v, gi, lf, states, gnorm, L):
    rowi = lax.broadcasted_iota(jnp.int32, (L, L), 0)
    coli = lax.broadcasted_iota(jnp.int32, (L, L), 1)
    causal = rowi >= coli
    b_cols = _sel_left(_one_hot(causal), lf)
    gi_t = gi.T[0:SUBLANES]
    lf_t = lf.T[0:SUBLANES]
    b_rows = _sel_right(lf_t, _one_hot(rowi <= coli))
    scale = DK ** -0.5
    ys, new_states = [], []
    for h in range(HEADS):
        c_mat, n_vec, m_prev = states[h]
        qh = q[:, h * DK:(h + 1) * DK]
        kh = k[:, h * DK:(h + 1) * DK]
        vh = v[:, h * DV:(h + 1) * DV]
        i_col = gi[:, h:h + 1]
        b_col = b_cols[:, HEADS + h:HEADS + h + 1]
        c_row = gi_t[h:h + 1, :] - b_rows[HEADS + h:HEADS + h + 1, :]
        dm = jnp.where(causal, c_row, NEG)
        big_m = jnp.maximum(m_prev, jnp.max(dm, axis=-1, keepdims=True))
        pmat = jnp.exp(dm - big_m)
        w = (_nt(qh, kh) * scale) * pmat
        s_inter = jnp.exp(m_prev - big_m)
        num = _nn(w.astype(BF16), vh) + s_inter * _nt(qh, c_mat.astype(BF16))
        qn = jnp.sum(qh.astype(F32) * n_vec, axis=-1, keepdims=True)
        den = jnp.sum(w, axis=-1, keepdims=True) + s_inter * qn
        hh = num / jnp.maximum(jnp.abs(den), jnp.exp(-(b_col + big_m)))
        ys.append(_head_layernorm(hh, gnorm[:, h * DV:(h + 1) * DV]))
        m_last = big_m[L - 1:L, :]
        wk = jnp.exp((i_col - b_col) - m_last)
        decay = jnp.exp(m_prev - m_last)
        kw = kh.astype(F32) * (wk * scale)
        c_new = decay * c_mat + _tn(vh, kw.astype(BF16))
        n_new = decay * n_vec + jnp.sum(kw, axis=0, keepdims=True)
        m_new = b_col[L - 1:L, :] + m_last
        new_states.append((c_new, n_new, m_new))
    return ys, new_states


def _mlstm_prompt_body(q_ref, k_ref, v_ref, g_ref, bias_ref, gn_ref, y_ref, c_ref, n_ref, m_ref, *, L, NC):
    @pl.when(pl.program_id(1) == 0)
    def _():
        c_ref[...] = jnp.zeros_like(c_ref)
        n_ref[...] = jnp.zeros_like(n_ref)
        m_ref[...] = jnp.zeros_like(m_ref)

    gates = g_ref[...] + bias_ref[...]
    lf_all = _log_sigmoid(gates)
    gnorm = gn_ref[...]
    for c in range(NC):
        sl = slice(c * L, (c + 1) * L)
        states = [(c_ref[0, h], n_ref[0, h:h + 1, :], m_ref[0, h:h + 1, 0:1]) for h in range(HEADS)]
        ys, new = _mlstm_chunk(q_ref[sl, :], k_ref[sl, :], v_ref[sl, :], gates[sl], lf_all[sl], states, gnorm, L)
        for h in range(HEADS):
            y_ref[sl, h * DV:(h + 1) * DV] = ys[h].astype(BF16)
            c_ref[0, h] = new[h][0]
            n_ref[0, h:h + 1, :] = new[h][1]
            m_ref[0, h:h + 1, :] = jnp.broadcast_to(new[h][2], (1, LANES))


def _mlstm_prompt(p, gates, bias, gnorm, *, nb, t_len):
    tb = MIX_TB
    nblk = t_len // tb
    rows = lambda width, col: pl.BlockSpec((tb, width), lambda b, t, col=col: (b * nblk + t, col))
    vec = lambda width: pl.BlockSpec((1, width), lambda b, t: (0, 0))
    return pl.pallas_call(
        functools.partial(_mlstm_prompt_body, L=ML_CHUNK, NC=tb // ML_CHUNK),
        grid=(nb, nblk),
        in_specs=[rows(512, C512_MQ), rows(512, C512_MK), rows(1024, C1K_MV), rows(LANES, 0), vec(LANES), vec(D_MODEL)],
        out_specs=(rows(D_MODEL, 0),
                   pl.BlockSpec((1, HEADS, DV, DK), lambda b, t: (b, 0, 0, 0)),
                   pl.BlockSpec((1, HEADS, DK), lambda b, t: (b, 0, 0)),
                   pl.BlockSpec((1, HEADS, LANES), lambda b, t: (b, 0, 0))),
        out_shape=(jax.ShapeDtypeStruct((nb * t_len, D_MODEL), BF16),
                   jax.ShapeDtypeStruct((nb, HEADS, DV, DK), F32),
                   jax.ShapeDtypeStruct((nb, HEADS, DK), F32),
                   jax.ShapeDtypeStruct((nb, HEADS, LANES), F32)),
        compiler_params=_cparams(2, 40), name="mlstm_prompt",
    )(p, p, p, gates, bias, gnorm)


def _seq_masks(L, nseq):
    rowi = lax.broadcasted_iota(jnp.int32, (L, L), 0)
    coli = lax.broadcasted_iota(jnp.int32, (L, L), 1)
    same = (rowi >> TS_SHIFT) == (coli >> TS_SHIFT)
    sb = lax.broadcasted_iota(jnp.int32, (nseq, L), 0)
    sr = lax.broadcasted_iota(jnp.int32, (nseq, L), 1)
    first = _one_hot(sr == sb * TS)
    member = _one_hot((sr >> TS_SHIFT) == sb)
    return rowi, coli, same, first, member


def _mlstm_sample_body(q_ref, k_ref, v_ref, g_ref, bias_ref, gn_ref, c0_ref, n0_ref, nrep_ref, mrep_ref, *rest,
                       NB, has_prev):
    y_ref, c_ref, n_ref, m_ref = rest[-4:]
    L = NB * TS
    rowi, coli, same, first, member = _seq_masks(L, NB)
    causal = jnp.logical_and(same, rowi >= coli)
    gates = g_ref[...] + bias_ref[...]
    lf = _log_sigmoid(gates)
    b_cols = _sel_left(_one_hot(causal), lf)
    btot_cols = _sel_left(_one_hot(same), lf)
    gi_t = gates.T[0:SUBLANES]
    lf_t = lf.T[0:SUBLANES]
    b_rows = _sel_right(lf_t, _one_hot(jnp.logical_and(same, rowi <= coli)))
    gnorm = gn_ref[...]
    q_all = q_ref[...].astype(F32)
    k_all = k_ref[...].astype(F32)
    v_all = v_ref[...].astype(F32)
    m_rep = mrep_ref[...]
    lane = lax.broadcasted_iota(jnp.int32, (L, LANES), 1)
    upper = lax.broadcasted_iota(jnp.int32, (WIN, 1), 0) < TS
    scale = DK ** -0.5
    for h in range(HEADS):
        qf = q_all[:, h * DK:(h + 1) * DK]
        kf = k_all[:, h * DK:(h + 1) * DK]
        vf = v_all[:, h * DV:(h + 1) * DV]
        qh, kh, vh = qf.astype(BF16), kf.astype(BF16), vf.astype(BF16)
        i_col = gates[:, h:h + 1]
        b_col = b_cols[:, HEADS + h:HEADS + h + 1]
        c_row = gi_t[h:h + 1, :] - b_rows[HEADS + h:HEADS + h + 1, :]
        m_prev = m_rep[:, h:h + 1]
        dm = jnp.where(causal, c_row, NEG)
        big_m = jnp.maximum(m_prev, jnp.max(dm, axis=-1, keepdims=True))
        m_last = jnp.maximum(m_prev, jnp.max(jnp.where(same, c_row, NEG), axis=-1, keepdims=True))
        pmat = jnp.exp(dm - big_m)
        w = (_nt(qh, kh) * scale) * pmat
        s_inter = jnp.exp(m_prev - big_m)
        qc = []
        for wd in range(NB // 2):
            qw = qf[WIN * wd:WIN * (wd + 1)].astype(BF16)
            o0 = _nt(qw, c0_ref[0, 2 * wd, h].astype(BF16))
            o1 = _nt(qw, c0_ref[0, 2 * wd + 1, h].astype(BF16))
            qc.append(jnp.where(upper, o0, o1))
        num = _nn(w.astype(BF16), vh) + s_inter * jnp.concatenate(qc, axis=0)
        qn = jnp.sum(qf * nrep_ref[:, h * DK:(h + 1) * DK], axis=-1, keepdims=True)
        den = jnp.sum(w, axis=-1, keepdims=True) + s_inter * qn
        hh = num / jnp.maximum(jnp.abs(den), jnp.exp(-(b_col + big_m)))
        y_ref[:, h * DV:(h + 1) * DV] = _head_layernorm(hh, gnorm[:, h * DV:(h + 1) * DV]).astype(BF16)
        wk = jnp.exp((i_col - b_col) - m_last)
        decay_col = jnp.exp(m_prev - m_last)
        kw = kf * (wk * scale)
        m_new_col = btot_cols[:, HEADS + h:HEADS + h + 1] + m_last
        z = jnp.where(lane == 0, decay_col, jnp.where(lane == 1, m_new_col, 0.0))
        zs = _sel_left(first, z)
        n_ref[h] = zs[:, 0:1] * n0_ref[h] + _sel_left(member, kw)
        m_ref[h] = jnp.broadcast_to(zs[:, 1:2], (NB, LANES))
        for b in range(NB):
            wd, par = b // 2, b % 2
            valid = upper if par == 0 else jnp.logical_not(upper)
            vwin = jnp.where(valid, vf[WIN * wd:WIN * (wd + 1)], 0.0).astype(BF16)
            kwin = kw[WIN * wd:WIN * (wd + 1)].astype(BF16)
            c_ref[0, b, h] = zs[b:b + 1, 0:1] * c0_ref[0, b, h] + _tn(vwin, kwin)


def _mlstm_sample(p, gates, bias, gnorm, c0, n0_hm, n_rep, m_rep, c_prev, *, layer, n_seq):
    nbk = SAMPLE_NB
    rb = nbk * TS
    rows = lambda width, col: pl.BlockSpec((rb, width), lambda i, col=col: (i, col))
    vec = lambda width: pl.BlockSpec((1, width), lambda i: (0, 0))
    hm = pl.BlockSpec((HEADS, nbk, LANES), lambda i: (0, i, 0))
    has_prev = c_prev is not None
    in_specs = [rows(512, C512_MQ), rows(512, C512_MK), rows(1024, C1K_MV), rows(LANES, 0), vec(LANES), vec(D_MODEL),
                pl.BlockSpec((1, nbk, HEADS, DV, DK), lambda i: (layer, i, 0, 0, 0)),
                hm, rows(HEADS * DK, 0), rows(LANES, 0)]
    args = [p, p, p, gates, bias, gnorm, c0, n0_hm, n_rep, m_rep]
    aliases = {}
    if has_prev:
        in_specs.append(pl.BlockSpec(memory_space=pl.ANY))
        args.append(c_prev)
        aliases[len(args) - 1] = 1
    return pl.pallas_call(
        functools.partial(_mlstm_sample_body, NB=nbk, has_prev=has_prev),
        grid=(n_seq // nbk,),
        in_specs=in_specs,
        out_specs=(rows(D_MODEL, 0),
                   pl.BlockSpec((1, nbk, HEADS, DV, DK), lambda i: (layer, i, 0, 0, 0)),
                   hm, hm),
        out_shape=(jax.ShapeDtypeStruct((n_seq * TS, D_MODEL), BF16),
                   jax.ShapeDtypeStruct(c0.shape, F32),
                   jax.ShapeDtypeStruct((HEADS, n_seq, LANES), F32),
                   jax.ShapeDtypeStruct((HEADS, n_seq, LANES), F32)),
        input_output_aliases=aliases,
        compiler_params=_cparams(1, 56), name="mlstm_sample",
    )(*args)


def _ref_rows(b, m, L):
    if m >= SUBLANES:
        nb = L // m
        b3 = b.reshape(nb, m, LANES)
        r = b3[:, m // 2 - 1:m // 2, :]
        return jnp.broadcast_to(r, (nb, m, LANES)).reshape(L, LANES)
    bt = b.reshape(L // SUBLANES, SUBLANES, LANES)
    sub = lax.broadcasted_iota(jnp.int32, bt.shape, 1)
    out = None
    for j in range(SUBLANES // m):
        idx = j * m + m // 2 - 1
        rj = jnp.broadcast_to(bt[:, idx:idx + 1, :], bt.shape)
        out = rj if out is None else jnp.where(sub >= j * m, rj, out)
    return out.reshape(L, LANES)


def _gla_scores(qf, kf, b, L, top, xor_rc, row1):
    a_mat = jnp.zeros((L, L), F32)
    m = top
    while m >= 2:
        ref = _ref_rows(b, m, L)
        second = (row1 & (m // 2)) != 0
        f = jnp.exp(jnp.where(second, b - ref, ref - b))
        qt = jnp.where(second, qf * f, 0.0).astype(BF16)
        kt = jnp.where(second, 0.0, kf * f).astype(BF16)
        am = _nt(qt, kt)
        if m < L:
            am = jnp.where(xor_rc < m, am, 0.0)
        a_mat = a_mat + am
        m //= 2
    return a_mat


def _gla_chunk(q, k, v, la, states, gnorm, L):
    rowi = lax.broadcasted_iota(jnp.int32, (L, L), 0)
    coli = lax.broadcasted_iota(jnp.int32, (L, L), 1)
    tril = _one_hot(rowi >= coli)
    xor_rc = rowi ^ coli
    row1 = lax.broadcasted_iota(jnp.int32, (L, 1), 0)
    scale = DK ** -0.5
    ys, new_states = [], []
    for h in range(HEADS):
        s_mat = states[h]
        qf = q[:, h * DK:(h + 1) * DK].astype(F32) * scale
        kf = k[:, h * DK:(h + 1) * DK].astype(F32)
        vh = v[:, h * DV:(h + 1) * DV]
        b = _sel_left(tril, la[:, h * DK:(h + 1) * DK])
        a_mat = _gla_scores(qf, kf, b, L, L, xor_rc, row1)
        diag = jnp.sum(qf * kf, axis=-1, keepdims=True)
        o = (_nn(a_mat.astype(BF16), vh) + diag * vh.astype(F32)
             + _nn((qf * jnp.exp(b)).astype(BF16), s_mat.astype(BF16)))
        ys.append(_head_rmsnorm(o, gnorm[:, h * DV:(h + 1) * DV]))
        b_last = b[L - 1:L, :]
        kd = (kf * jnp.exp(b_last - b)).astype(BF16)
        e_col = jnp.broadcast_to(jnp.exp(b_last), (SUBLANES, LANES)).T[:, 0:1]
        new_states.append(e_col * s_mat + _tn(kd, vh))
    return ys, new_states


def _gla_log_decay(gates, w2_ref, b2_ref):
    return _log_sigmoid(_nn(gates.astype(BF16), w2_ref[...]) + b2_ref[...]) * (1.0 / GLA_TAU)


def _gla_prompt_body(q_ref, k_ref, v_ref, g_ref, w2_ref, b2_ref, gn_ref, y_ref, s_ref, *, L, NC):
    @pl.when(pl.program_id(1) == 0)
    def _():
        s_ref[...] = jnp.zeros_like(s_ref)

    la_all = _gla_log_decay(g_ref[...], w2_ref, b2_ref)
    gnorm = gn_ref[...]
    for c in range(NC):
        sl = slice(c * L, (c + 1) * L)
        states = [s_ref[0, h] for h in range(HEADS)]
        ys, new = _gla_chunk(q_ref[sl, :], k_ref[sl, :], v_ref[sl, :], la_all[sl], states, gnorm, L)
        for h in range(HEADS):
            y_ref[sl, h * DV:(h + 1) * DV] = ys[h].astype(BF16)
            s_ref[0, h] = new[h]


def _gla_prompt(p, gates, w2, b2, gnorm, *, nb, t_len):
    tb = MIX_TB
    nblk = t_len // tb
    rows = lambda width, col: pl.BlockSpec((tb, width), lambda b, t, col=col: (b * nblk + t, col))
    vec = lambda width: pl.BlockSpec((1, width), lambda b, t: (0, 0))
    return pl.pallas_call(
        functools.partial(_gla_prompt_body, L=GLA_CHUNK, NC=tb // GLA_CHUNK),
        grid=(nb, nblk),
        in_specs=[rows(512, C512_GQ), rows(512, C512_GK), rows(1024, C1K_GV), rows(LANES, 0),
                  pl.BlockSpec((LANES, 512), lambda b, t: (0, 0)), vec(512), vec(D_MODEL)],
        out_specs=(rows(D_MODEL, 0), pl.BlockSpec((1, HEADS, DK, DV), lambda b, t: (b, 0, 0, 0))),
        out_shape=(jax.ShapeDtypeStruct((nb * t_len, D_MODEL), BF16), jax.ShapeDtypeStruct((nb, HEADS, DK, DV), F32)),
        compiler_params=_cparams(2, 40), name="gla_prompt",
    )(p, p, p, gates, w2, b2, gnorm)


def _gla_sample_body(q_ref, k_ref, v_ref, g_ref, w2_ref, b2_ref, gn_ref, s0_ref, *rest, NB, has_prev):
    y_ref, s_ref = rest[-2:]
    L = NB * TS
    rowi, coli, same, first, _ = _seq_masks(L, NB)
    tril_seq = _one_hot(jnp.logical_and(same, rowi >= coli))
    full_seq = _one_hot(same)
    xor_rc = rowi ^ coli
    row1 = lax.broadcasted_iota(jnp.int32, (L, 1), 0)
    la_all = _gla_log_decay(g_ref[...], w2_ref, b2_ref)
    gnorm = gn_ref[...]
    q_all = q_ref[...].astype(F32)
    k_all = k_ref[...].astype(F32)
    v_all = v_ref[...].astype(F32)
    upper = lax.broadcasted_iota(jnp.int32, (WIN, 1), 0) < TS
    scale = DK ** -0.5
    for h in range(HEADS):
        qf = q_all[:, h * DK:(h + 1) * DK] * scale
        kf = k_all[:, h * DK:(h + 1) * DK]
        vf = v_all[:, h * DV:(h + 1) * DV]
        vh = vf.astype(BF16)
        la = la_all[:, h * DK:(h + 1) * DK]
        b = _sel_left(tril_seq, la)
        btot = _sel_left(full_seq, la)
        a_mat = _gla_scores(qf, kf, b, L, TS, xor_rc, row1)
        diag = jnp.sum(qf * kf, axis=-1, keepdims=True)
        qe = qf * jnp.exp(b)
        inter = []
        for wd in range(NB // 2):
            qw = qe[WIN * wd:WIN * (wd + 1)].astype(BF16)
            o0 = _nn(qw, s0_ref[0, 2 * wd, h].astype(BF16))
            o1 = _nn(qw, s0_ref[0, 2 * wd + 1, h].astype(BF16))
            inter.append(jnp.where(upper, o0, o1))
        o = _nn(a_mat.astype(BF16), vh) + diag * vf + jnp.concatenate(inter, axis=0)
        y_ref[:, h * DV:(h + 1) * DV] = _head_rmsnorm(o, gnorm[:, h * DV:(h + 1) * DV]).astype(BF16)
        kd = kf * jnp.exp(btot - b)
        e_cols = jnp.exp(_sel_left(first, btot)).T
        for bq in range(NB):
            wd, par = bq // 2, bq % 2
            valid = upper if par == 0 else jnp.logical_not(upper)
            kwin = jnp.where(valid, kd[WIN * wd:WIN * (wd + 1)], 0.0).astype(BF16)
            vwin = vf[WIN * wd:WIN * (wd + 1)].astype(BF16)
            s_ref[0, bq, h] = e_cols[:, bq:bq + 1] * s0_ref[0, bq, h] + _tn(kwin, vwin)


def _gla_sample(p, gates, w2, b2, gnorm, s0, s_prev, *, layer, n_seq):
    nbk = SAMPLE_NB
    rb = nbk * TS
    rows = lambda width, col: pl.BlockSpec((rb, width), lambda i, col=col: (i, col))
    vec = lambda width: pl.BlockSpec((1, width), lambda i: (0, 0))
    has_prev = s_prev is not None
    in_specs = [rows(512, C512_GQ), rows(512, C512_GK), rows(1024, C1K_GV), rows(LANES, 0),
                pl.BlockSpec((LANES, 512), lambda i: (0, 0)), vec(512), vec(D_MODEL),
                pl.BlockSpec((1, nbk, HEADS, DK, DV), lambda i: (layer, i, 0, 0, 0))]
    args = [p, p, p, gates, w2, b2, gnorm, s0]
    aliases = {}
    if has_prev:
        in_specs.append(pl.BlockSpec(memory_space=pl.ANY))
        args.append(s_prev)
        aliases[len(args) - 1] = 1
    return pl.pallas_call(
        functools.partial(_gla_sample_body, NB=nbk, has_prev=has_prev),
        grid=(n_seq // nbk,),
        in_specs=in_specs,
        out_specs=(rows(D_MODEL, 0), pl.BlockSpec((1, nbk, HEADS, DK, DV), lambda i: (layer, i, 0, 0, 0))),
        out_shape=(jax.ShapeDtypeStruct((n_seq * TS, D_MODEL), BF16), jax.ShapeDtypeStruct(s0.shape, F32)),
        input_output_aliases=aliases,
        compiler_params=_cparams(1, 56), name="gla_sample",
    )(*args)


def _lru_gates(xc, wa_ref, ba_ref, wi_ref, bi_ref, lam_ref):
    ra, ri = [], []
    for nb in range(LRU_BLOCKS):
        xs = xc[:, nb * LRU_BW:(nb + 1) * LRU_BW].astype(BF16)
        ra.append(_nn(xs, wa_ref[nb]))
        ri.append(_nn(xs, wi_ref[nb]))
    r = _sigmoid(jnp.concatenate(ra, axis=-1) + ba_ref[...])
    ig = _sigmoid(jnp.concatenate(ri, axis=-1) + bi_ref[...])
    a = jnp.exp(((-LRU_C) * _softplus(-lam_ref[...])) * r)
    om = 1.0 - a * a
    mult = om * lax.rsqrt(jnp.maximum(om, 1e-30))
    return a, ig * xc, mult


def _conv_shift_matrices(tb):
    nsh = CONV_W - 1
    j = jnp.arange(nsh)[:, None, None] + 1
    t = jnp.arange(tb)[None, :, None]
    s = jnp.arange(tb)[None, None, :]
    shift = (t - s == j).astype(BF16).reshape(nsh * tb, tb)
    t8 = jnp.arange(SUBLANES)[None, :, None]
    s8 = jnp.arange(SUBLANES)[None, None, :]
    tail = (s8 - t8 == SUBLANES - j).astype(BF16).reshape(nsh * SUBLANES, SUBLANES)
    return shift, tail


def _lru_prompt_body(x_ref, shift_ref, tail_ref, wc_ref, bc_ref, wa_ref, ba_ref, wi_ref, bi_ref, lam_ref,
                     h_ref, hfin_ref, cfin_ref, xprev, hcar, *, TB):
    t = pl.program_id(1)

    @pl.when(t == 0)
    def _():
        xprev[...] = jnp.zeros_like(xprev)
        hcar[...] = jnp.zeros_like(hcar)

    xb = x_ref[...]
    x = xb.astype(F32)
    nsh = CONV_W - 1
    shifted = _nn(shift_ref[...], xb)
    from_prev = _nn(tail_ref[...], xprev[...].astype(BF16))
    wc = wc_ref[...]
    xc = bc_ref[...] + wc[CONV_W - 1:CONV_W, :] * x
    for j in range(nsh):
        sj = shifted[j * TB:(j + 1) * TB]
        sj = jnp.concatenate([sj[0:SUBLANES] + from_prev[j * SUBLANES:(j + 1) * SUBLANES], sj[SUBLANES:]], axis=0)
        xc = xc + wc[CONV_W - 2 - j:CONV_W - 1 - j, :] * sj
    xprev[...] = x[TB - SUBLANES:TB, :]

    a, gx, mult = _lru_gates(xc, wa_ref, ba_ref, wi_ref, bi_ref, lam_ref)
    row8 = lax.broadcasted_iota(jnp.int32, (SUBLANES, 1), 0)
    m_first = jnp.where(row8 + t == 0, 1.0, mult[0:SUBLANES])
    u = jnp.concatenate([m_first, mult[SUBLANES:]], axis=0) * gx
    nt8 = TB // SUBLANES
    a3 = a.reshape(nt8, SUBLANES, D_MODEL)
    u3 = u.reshape(nt8, SUBLANES, D_MODEL)
    sub = lax.broadcasted_iota(jnp.int32, (1, SUBLANES, D_MODEL), 1)
    d = 1
    while d < SUBLANES:
        keep = sub >= d
        a_sh = jnp.where(keep, pltpu.roll(a3, d, 1), 1.0)
        u_sh = jnp.where(keep, pltpu.roll(u3, d, 1), 0.0)
        u3 = a3 * u_sh + u3
        a3 = a3 * a_sh
        d *= 2
    h_in = hcar[...]
    hs = []
    for j in range(nt8):
        hj = a3[j] * h_in + u3[j]
        hs.append(hj)
        h_in = hj[SUBLANES - 1:SUBLANES]
    h_ref[...] = jnp.concatenate(hs, axis=0).astype(BF16)
    hcar[...] = h_in
    hfin_ref[0] = h_in
    cfin_ref[0] = x[TB - (CONV_W - 1):TB, :]


def _lru_prompt(p, wc, bc, wa, ba, wi, bi, lam, *, nb, t_len):
    tb = LRU_TB
    nblk = t_len // tb
    vec = pl.BlockSpec((1, D_MODEL), lambda b, t: (0, 0))
    wblk = pl.BlockSpec((LRU_BLOCKS, LRU_BW, LRU_BW), lambda b, t: (0, 0, 0))
    shift, tail = _conv_shift_matrices(tb)
    whole = lambda arr: pl.BlockSpec(arr.shape, lambda b, t: (0, 0))
    return pl.pallas_call(
        functools.partial(_lru_prompt_body, TB=tb),
        grid=(nb, nblk),
        in_specs=[pl.BlockSpec((tb, D_MODEL), lambda b, t: (b * nblk + t, C1K_LX)), whole(shift), whole(tail),
                  pl.BlockSpec((CONV_W, D_MODEL), lambda b, t: (0, 0)), vec, wblk, vec, wblk, vec, vec],
        out_specs=(pl.BlockSpec((tb, D_MODEL), lambda b, t: (b * nblk + t, 0)),
                   pl.BlockSpec((1, 1, D_MODEL), lambda b, t: (b, 0, 0)),
                   pl.BlockSpec((1, CONV_W - 1, D_MODEL), lambda b, t: (b, 0, 0))),
        out_shape=(jax.ShapeDtypeStruct((nb * t_len, D_MODEL), BF16),
                   jax.ShapeDtypeStruct((nb, 1, D_MODEL), F32),
                   jax.ShapeDtypeStruct((nb, CONV_W - 1, D_MODEL), F32)),
        scratch_shapes=[pltpu.VMEM((SUBLANES, D_MODEL), F32), pltpu.VMEM((1, D_MODEL), F32)],
        compiler_params=_cparams(2, 40), name="lru_prompt",
    )(p, shift, tail, wc, bc, wa, ba, wi, bi, lam)


def _lru_sample_body(x_ref, buf_ref, h0_ref, wc_ref, bc_ref, wa_ref, ba_ref, wi_ref, bi_ref, lam_ref,
                     h_ref, hfin_ref, cfin_ref, *, T):
    wc = wc_ref[...]
    xs = [buf_ref[j] for j in range(CONV_W - 1)] + [x_ref[t].astype(F32) for t in range(T)]
    h = h0_ref[...]
    for t in range(T):
        xc = bc_ref[...]
        for j in range(CONV_W):
            xc = xc + wc[j:j + 1, :] * xs[t + j]
        a, gx, mult = _lru_gates(xc, wa_ref, ba_ref, wi_ref, bi_ref, lam_ref)
        h = a * h + mult * gx
        h_ref[t] = h.astype(BF16)
    hfin_ref[...] = h
    for j in range(CONV_W - 1):
        cfin_ref[j] = xs[T + j]


def _lru_sample(x_tm, buf_tm, h0, wc, bc, wa, ba, wi, bi, lam):
    t_len, n_seq, _ = x_tm.shape
    return pl.pallas_call(
        functools.partial(_lru_sample_body, T=t_len),
        out_shape=(jax.ShapeDtypeStruct((t_len, n_seq, D_MODEL), BF16),
                   jax.ShapeDtypeStruct((n_seq, D_MODEL), F32),
                   jax.ShapeDtypeStruct((CONV_W - 1, n_seq, D_MODEL), F32)),
        compiler_params=pltpu.CompilerParams(vmem_limit_bytes=40 * 1024 * 1024), name="lru_sample",
    )(x_tm, buf_tm, h0, wc, bc, wa, ba, wi, bi, lam)


def _proj_weights(w):
    col = lambda r: w[:, r[0]:r[1]]
    w_big = jnp.concatenate([col(W_IN_A), col(W_IN_B), col(W_IN_C)], axis=1).astype(BF16)
    pad = jnp.zeros((D_MODEL, LANES - 2 * HEADS - GLA_RANK), F32)
    w_small = jnp.concatenate([col(W_IN_IF), col(W_IN_LR), pad], axis=1).astype(BF16)
    return w_big, w_small


def kernel(x_prompt, x_sample, state_mlstm_C, state_mlstm_n, state_mlstm_m, state_gla_S, state_lru_h, state_lru_conv,
           norm_ffn1, w_ffn1_in, w_ffn1_out, norm_mix, w_in, b_ml_if, g_ml_norm, w_gla_lr2, b_gla_gate, g_gla_norm,
           w_conv, b_conv, w_lru_a, b_lru_a, w_lru_i, b_lru_i, lru_lambda, w_out, norm_ffn2, w_ffn2_in, w_ffn2_out,
           norm_final):
    nb, t_len, d = x_prompt.shape
    n_seq, ts, _ = x_sample.shape
    depth = w_in.shape[0]
    assert d == D_MODEL and ts == TS and t_len % MIX_TB == 0 and n_seq % SAMPLE_NB == 0

    xp = x_prompt.reshape(nb * t_len, d)
    xs = x_sample.reshape(n_seq * ts, d)

    new_p = {k: [] for k in ("C", "n", "m", "S", "h", "conv")}
    new_s = {k: [] for k in ("n", "m", "h", "conv")}
    s_c_all, s_s_all = None, None
    yp = ys = None
    for l in range(depth):
        w_big, w_small = _proj_weights(w_in[l])
        bias_if = jnp.concatenate([b_ml_if[l], jnp.zeros((LANES - 2 * HEADS,), F32)]).reshape(1, LANES)
        w2 = jnp.zeros((LANES, HEADS * DK), F32).at[2 * HEADS:2 * HEADS + GLA_RANK].set(w_gla_lr2[l]).astype(BF16)
        b2 = b_gla_gate[l].reshape(1, HEADS * DK)
        g_ml = g_ml_norm[l].reshape(1, d)
        g_gla = g_gla_norm[l].reshape(1, d)
        wc, bc = w_conv[l], b_conv[l].reshape(1, d)
        wa, wi = w_lru_a[l].astype(BF16), w_lru_i[l].astype(BF16)
        ba, bi, lam = b_lru_a[l].reshape(1, d), b_lru_i[l].reshape(1, d), lru_lambda[l].reshape(1, d)
        f1_in, f1_out = w_ffn1_in[l].astype(BF16), w_ffn1_out[l].astype(BF16)
        f2_in, f2_out = w_ffn2_in[l].astype(BF16), w_ffn2_out[l].astype(BF16)
        w_o = w_out[l].astype(BF16)
        last = l == depth - 1
        g_next = norm_final if last else norm_ffn1[l + 1]

        x1, xn = _ffn(xp, norm_ffn1[l], f1_in, f1_out, norm_mix[l], final=False)
        p, gates = _proj(xn, w_big, w_small)
        y_ml, p_c, p_n, p_m = _mlstm_prompt(p, gates, bias_if, g_ml, nb=nb, t_len=t_len)
        y_gla, p_s = _gla_prompt(p, gates, w2, b2, g_gla, nb=nb, t_len=t_len)
        h_lru, p_h, p_conv = _lru_prompt(p, wc, bc, wa, ba, wi, bi, lam, nb=nb, t_len=t_len)
        x2 = _merge(x1, p, y_ml, y_gla, h_lru, w_o)
        if last:
            yp = _ffn(x2, norm_ffn2[l], f2_in, f2_out, g_next, final=True)
        else:
            xp, _ = _ffn(x2, norm_ffn2[l], f2_in, f2_out, g_next, final=False)

        x1, xn = _ffn(xs, norm_ffn1[l], f1_in, f1_out, norm_mix[l], final=False)
        p, gates = _proj(xn, w_big, w_small)
        n0_hm = state_mlstm_n[l].transpose(1, 0, 2)
        n_rep = jnp.repeat(state_mlstm_n[l].reshape(n_seq, HEADS * DK), ts, axis=0)
        m_rep = jnp.repeat(jnp.pad(state_mlstm_m[l], ((0, 0), (0, LANES - HEADS))), ts, axis=0)
        y_ml, s_c_all, s_n, s_m = _mlstm_sample(p, gates, bias_if, g_ml, state_mlstm_C, n0_hm, n_rep, m_rep,
                                                s_c_all, layer=l, n_seq=n_seq)
        y_gla, s_s_all = _gla_sample(p, gates, w2, b2, g_gla, state_gla_S, s_s_all, layer=l, n_seq=n_seq)
        lx_s = p[:, C1K_LX * 1024:(C1K_LX + 1) * 1024].reshape(n_seq, ts, d).transpose(1, 0, 2)
        hs_tm, s_h, s_conv_tm = _lru_sample(lx_s, state_lru_conv[l].transpose(1, 0, 2), state_lru_h[l],
                                            wc, bc, wa, ba, wi, bi, lam)
        h_lru = hs_tm.transpose(1, 0, 2).reshape(n_seq * ts, d)
        x2 = _merge(x1, p, y_ml, y_gla, h_lru, w_o)
        if last:
            ys = _ffn(x2, norm_ffn2[l], f2_in, f2_out, g_next, final=True)
        else:
            xs, _ = _ffn(x2, norm_ffn2[l], f2_in, f2_out, g_next, final=False)

        new_p["C"].append(p_c)
        new_p["n"].append(p_n)
        new_p["m"].append(p_m[:, :, 0])
        new_p["S"].append(p_s)
        new_p["h"].append(p_h[:, 0, :])
        new_p["conv"].append(p_conv)
        new_s["n"].append(s_n.transpose(1, 0, 2))
        new_s["m"].append(s_m[:, :, 0].T)
        new_s["h"].append(s_h)
        new_s["conv"].append(s_conv_tm.transpose(1, 0, 2))

    st = lambda arrs: jnp.stack(arrs)
    return (yp.reshape(nb, t_len, d), ys.reshape(n_seq, ts, d),
            st(new_p["C"]), st(new_p["n"]), st(new_p["m"]), st(new_p["S"]), st(new_p["h"]), st(new_p["conv"]),
            s_c_all, st(new_s["n"]), st(new_s["m"]), s_s_all, st(new_s["h"]), st(new_s["conv"]))
```

```python
import functools

import jax
import jax.numpy as jnp
from jax import lax
from jax.experimental import pallas as pl
from jax.experimental.pallas import tpu as pltpu

F32 = jnp.float32
BF16 = jnp.bfloat16

D_MODEL = 1024
D_FF = 2816
HEADS = 4
DK = 128
DV = 256
GLA_RANK = 16
GLA_TAU = 16.0
LRU_BLOCKS = 8
LRU_BW = 128
LRU_C = 8.0
CONV_W = 4
EPS = 1e-6
NEG = -1e30

LANES = 128
SUBLANES = 8
MXU_N = 256

P_COLS = 11264
C512_MQ, C512_MK, C512_GQ, C512_GK = 0, 1, 6, 7
C1K_MV, C1K_MO, C1K_GV, C1K_GG, C1K_LX, C1K_LY, C1K_MG0 = 1, 2, 4, 5, 6, 7, 8
W_IN_A = (0, 3072)
W_IN_IF = (3072, 3080)
W_IN_B = (3080, 6152)
W_IN_LR = (6152, 6168)
W_IN_C = (6168, 11288)

ML_CHUNK = 256
GLA_CHUNK = 128
MIX_TB = 512
LRU_TB = 256
TS = 4
TS_SHIFT = 2
SAMPLE_NB = 16
WIN = 8
FF_CHUNK = MXU_N


def _cparams(n_axes, vmem_mib):
    return pltpu.CompilerParams(dimension_semantics=("arbitrary",) * n_axes,
                                vmem_limit_bytes=vmem_mib * 1024 * 1024)


def _pick_tile(n, target, mult):
    best = None
    for t in range(mult, min(n, target) + 1, mult):
        if n % t == 0:
            best = t
    assert best is not None, (n, target, mult)
    return best


def _sigmoid(x):
    return 0.5 * jnp.tanh(0.5 * x) + 0.5


def _log_sigmoid(x):
    return jnp.minimum(x, 0.0) - jnp.log(1.0 + jnp.exp(-jnp.abs(x)))


def _softplus(x):
    return jnp.maximum(x, 0.0) + jnp.log(1.0 + jnp.exp(-jnp.abs(x)))


def _rms(x, g):
    return x * lax.rsqrt(jnp.mean(x * x, axis=-1, keepdims=True) + EPS) * g


def _nn(a, b):
    return jnp.dot(a, b, preferred_element_type=F32)


def _nt(a, b):
    return lax.dot_general(a, b, (((1,), (1,)), ((), ())), preferred_element_type=F32)


def _tn(a, b):
    return lax.dot_general(a, b, (((0,), (0,)), ((), ())), preferred_element_type=F32)


def _split3(x):
    hi = x.astype(BF16)
    r1 = x - hi.astype(F32)
    mid = r1.astype(BF16)
    lo = (r1 - mid.astype(F32)).astype(BF16)
    return hi, mid, lo


def _sel_left(sel, x):
    hi, mid, lo = _split3(x)
    return _nn(sel, hi) + _nn(sel, mid) + _nn(sel, lo)


def _sel_right(x, sel):
    hi, mid, lo = _split3(x)
    return _nn(hi, sel) + _nn(mid, sel) + _nn(lo, sel)


def _one_hot(mask):
    return jnp.where(mask, 1.0, 0.0).astype(BF16)


def _head_layernorm(h, g):
    mu = jnp.mean(h, axis=-1, keepdims=True)
    hc = h - mu
    return hc * lax.rsqrt(jnp.mean(hc * hc, axis=-1, keepdims=True) + EPS) * g


def _head_rmsnorm(h, g):
    return h * lax.rsqrt(jnp.mean(h * h, axis=-1, keepdims=True) + EPS) * g


def _ffn_body(x_ref, g_ref, win_ref, wout_ref, gn_ref, *rest, n_chunks, final):
    if final:
        y_ref, xn_s, acc_s = rest
    else:
        y_ref, xn_ref, xn_s, acc_s = rest
    xn_s[...] = _rms(x_ref[...], g_ref[...]).astype(BF16)
    for c in range(n_chunks):
        xn = xn_s[...]
        g = _nn(xn, win_ref[:, c * FF_CHUNK:(c + 1) * FF_CHUNK])
        u = _nn(xn, win_ref[:, D_FF + c * FF_CHUNK:D_FF + (c + 1) * FF_CHUNK])
        t = jnp.tanh(0.5 * g)
        a = (g * (t + 1.0)) * u
        part = _nn(a.astype(BF16), wout_ref[c * FF_CHUNK:(c + 1) * FF_CHUNK, :])
        if c == 0:
            acc_s[...] = part
        else:
            acc_s[...] += part
    xo = x_ref[...] + 0.25 * acc_s[...]
    if final:
        y_ref[...] = _rms(xo, gn_ref[...])
    else:
        y_ref[...] = xo
        xn_ref[...] = _rms(xo, gn_ref[...]).astype(BF16)


def _ffn(x, g, w_in_c, w_out_c, g_next, *, final):
    rows = x.shape[0]
    tm = _pick_tile(rows, 1024, 16)
    n_chunks = D_FF // FF_CHUNK
    resident = lambda shape: pl.BlockSpec(shape, lambda i: (0,) * len(shape), pipeline_mode=pl.Buffered(1))
    row_spec = pl.BlockSpec((tm, D_MODEL), lambda i: (i, 0))
    vec = pl.BlockSpec((1, D_MODEL), lambda i: (0, 0))
    if final:
        out_shape = jax.ShapeDtypeStruct((rows, D_MODEL), F32)
        out_specs = row_spec
    else:
        out_shape = (jax.ShapeDtypeStruct((rows, D_MODEL), F32), jax.ShapeDtypeStruct((rows, D_MODEL), BF16))
        out_specs = (row_spec, row_spec)
    return pl.pallas_call(
        functools.partial(_ffn_body, n_chunks=n_chunks, final=final),
        grid=(rows // tm,),
        in_specs=[row_spec, vec, resident(w_in_c.shape), resident(w_out_c.shape), vec],
        out_specs=out_specs, out_shape=out_shape,
        scratch_shapes=[pltpu.VMEM((tm, D_MODEL), BF16), pltpu.VMEM((tm, D_MODEL), F32)],
        compiler_params=_cparams(1, 56), name="ffn_final" if final else "ffn",
    )(x, g.reshape(1, D_MODEL), w_in_c, w_out_c, g_next.reshape(1, D_MODEL))


def _proj_body(xn_ref, w_ref, ws_ref, p_ref, g_ref):
    xn = xn_ref[...]
    p_ref[...] = _nn(xn, w_ref[...]).astype(BF16)

    @pl.when(pl.program_id(1) == 0)
    def _():
        g_ref[...] = _nn(xn, ws_ref[...])


def _proj(xn, w_big, w_small):
    rows = xn.shape[0]
    tm = _pick_tile(rows, 2048, 16)
    tn = 1024
    return pl.pallas_call(
        _proj_body, grid=(rows // tm, P_COLS // tn),
        in_specs=[pl.BlockSpec((tm, D_MODEL), lambda i, j: (i, 0)),
                  pl.BlockSpec((D_MODEL, tn), lambda i, j: (0, j)),
                  pl.BlockSpec((D_MODEL, LANES), lambda i, j: (0, 0))],
        out_specs=(pl.BlockSpec((tm, tn), lambda i, j: (i, j)),
                   pl.BlockSpec((tm, LANES), lambda i, j: (i, 0))),
        out_shape=(jax.ShapeDtypeStruct((rows, P_COLS), BF16), jax.ShapeDtypeStruct((rows, LANES), F32)),
        compiler_params=_cparams(2, 48), name="proj",
    )(xn, w_big, w_small)


def _merge_body(x_ref, mo_ref, gg_ref, ly_ref, m0_ref, m1_ref, m2_ref, yml_ref, ygla_ref, hlru_ref, w_ref, o_ref):
    f = lambda r: r[...].astype(F32)
    th = lambda r: jnp.tanh(0.5 * f(r)) + 1.0
    gg = f(gg_ref)
    ly = f(ly_ref)
    y_ml = f(yml_ref) * th(mo_ref)
    y_gla = f(ygla_ref) * (gg * th(gg_ref))
    y_lru = f(hlru_ref) * (ly * (1.0 + jnp.tanh(0.7978845608028654 * (ly + 0.044715 * (ly * ly * ly)))))
    merged4 = th(m0_ref) * y_ml + th(m1_ref) * y_gla + th(m2_ref) * y_lru
    o_ref[...] = x_ref[...] + _nn((0.25 * merged4).astype(BF16), w_ref[...])


def _merge(x, p, y_ml, y_gla, h_lru, w_out):
    rows = x.shape[0]
    tm = _pick_tile(rows, 512, 16)
    row = lambda c: pl.BlockSpec((tm, D_MODEL), lambda i, c=c: (i, c))
    return pl.pallas_call(
        _merge_body, grid=(rows // tm,),
        in_specs=[row(0), row(C1K_MO), row(C1K_GG), row(C1K_LY), row(C1K_MG0), row(C1K_MG0 + 1), row(C1K_MG0 + 2),
                  row(0), row(0), row(0), pl.BlockSpec((D_MODEL, D_MODEL), lambda i: (0, 0))],
        out_specs=row(0),
        out_shape=jax.ShapeDtypeStruct((rows, D_MODEL), F32),
        compiler_params=_cparams(1, 56), name="merge",
    )(x, p, p, p, p, p, p, y_ml, y_gla, h_lru, w_out)


def _mlstm_chunk(q, k, v, gi, lf, states, g_rep, L):
    rowi = lax.broadcasted_iota(jnp.int32, (L, L), 0)
    coli = lax.broadcasted_iota(jnp.int32, (L, L), 1)
    upper = rowi <= coli
    b_cols = _sel_left(_one_hot(rowi >= coli), lf)
    lf_t = lf.T[0:SUBLANES]
    b_rows = _sel_right(lf_t, _one_hot(upper))
    scale = DK ** -0.5
    ys, new_states = [], []
    for h in range(HEADS):
        c_mat, n_vec, m_prev = states[h]
        qh = q[:, h * DK:(h + 1) * DK]
        kh = k[:, h * DK:(h + 1) * DK]
        vh = v[:, h * DV:(h + 1) * DV]
        b_col = b_cols[:, HEADS + h:HEADS + h + 1]
        b_row = b_rows[HEADS + h:HEADS + h + 1, :]
        c_col = gi[:, h:h + 1] - b_col
        dm = jnp.where(upper, c_col, NEG)
        big_m = jnp.maximum(m_prev, jnp.max(dm, axis=0, keepdims=True))
        w = (_nt(kh, qh) * scale) * jnp.exp(dm - big_m)
        s_inter = jnp.exp(m_prev - big_m)
        num = _tn(vh, w.astype(BF16)) + s_inter * _nt(c_mat.astype(BF16), qh)
        qn = _nt(jnp.broadcast_to(n_vec, (SUBLANES, DK)).astype(BF16), qh)[0:1, :]
        den = jnp.sum(w, axis=0, keepdims=True) + s_inter * qn
        hh = num / jnp.maximum(jnp.abs(den), jnp.exp(-(b_row + big_m)))
        mu = jnp.mean(hh, axis=0, keepdims=True)
        hc = hh - mu
        g_h = jnp.concatenate([g_rep[h * DV:(h + 1) * DV]] * (L // LANES), axis=1)
        y_t = hc * lax.rsqrt(jnp.mean(hc * hc, axis=0, keepdims=True) + EPS) * g_h
        ys.append(y_t.astype(BF16).T)
        m_last = big_m[:, L - 1:L]
        wk = jnp.exp(c_col - m_last)
        decay = jnp.exp(m_prev - m_last)
        kw = kh.astype(F32) * (wk * scale)
        c_new = decay * c_mat + _tn(vh, kw.astype(BF16))
        n_new = decay * n_vec + jnp.sum(kw, axis=0, keepdims=True)
        m_new = b_col[L - 1:L, :] + m_last
        new_states.append((c_new, n_new, m_new))
    return ys, new_states


def _mlstm_prompt_body(q_ref, k_ref, v_ref, g_ref, bias_ref, gn_ref, y_ref, c_ref, n_ref, m_ref, *, L, NC):
    @pl.when(pl.program_id(1) == 0)
    def _():
        c_ref[...] = jnp.zeros_like(c_ref)
        n_ref[...] = jnp.zeros_like(n_ref)
        m_ref[...] = jnp.zeros_like(m_ref)

    gates = g_ref[...] + bias_ref[...]
    lf_all = _log_sigmoid(gates)
    gnorm = gn_ref[...]
    for c in range(NC):
        sl = slice(c * L, (c + 1) * L)
        states = [(c_ref[0, h], n_ref[0, h:h + 1, :], m_ref[0, h:h + 1, 0:1]) for h in range(HEADS)]
        ys, new = _mlstm_chunk(q_ref[sl, :], k_ref[sl, :], v_ref[sl, :], gates[sl], lf_all[sl], states, gnorm, L)
        for h in range(HEADS):
            y_ref[sl, h * DV:(h + 1) * DV] = ys[h]
            c_ref[0, h] = new[h][0]
            n_ref[0, h:h + 1, :] = new[h][1]
            m_ref[0, h:h + 1, :] = jnp.broadcast_to(new[h][2], (1, LANES))


def _mlstm_prompt(p, gates, bias, g_rep, *, nb, t_len):
    tb = MIX_TB
    nblk = t_len // tb
    rows = lambda width, col: pl.BlockSpec((tb, width), lambda b, t, col=col: (b * nblk + t, col))
    vec = lambda width: pl.BlockSpec((1, width), lambda b, t: (0, 0))
    return pl.pallas_call(
        functools.partial(_mlstm_prompt_body, L=ML_CHUNK, NC=tb // ML_CHUNK),
        grid=(nb, nblk),
        in_specs=[rows(512, C512_MQ), rows(512, C512_MK), rows(1024, C1K_MV), rows(LANES, 0), vec(LANES),
                  pl.BlockSpec((D_MODEL, LANES), lambda b, t: (0, 0))],
        out_specs=(rows(D_MODEL, 0),
                   pl.BlockSpec((1, HEADS, DV, DK), lambda b, t: (b, 0, 0, 0)),
                   pl.BlockSpec((1, HEADS, DK), lambda b, t: (b, 0, 0)),
                   pl.BlockSpec((1, HEADS, LANES), lambda b, t: (b, 0, 0))),
        out_shape=(jax.ShapeDtypeStruct((nb * t_len, D_MODEL), BF16),
                   jax.ShapeDtypeStruct((nb, HEADS, DV, DK), F32),
                   jax.ShapeDtypeStruct((nb, HEADS, DK), F32),
                   jax.ShapeDtypeStruct((nb, HEADS, LANES), F32)),
        compiler_params=_cparams(2, 40), name="mlstm_prompt",
    )(p, p, p, gates, bias, g_rep)


def _seq_masks(L, nseq):
    rowi = lax.broadcasted_iota(jnp.int32, (L, L), 0)
    coli = lax.broadcasted_iota(jnp.int32, (L, L), 1)
    same = (rowi >> TS_SHIFT) == (coli >> TS_SHIFT)
    sb = lax.broadcasted_iota(jnp.int32, (nseq, L), 0)
    sr = lax.broadcasted_iota(jnp.int32, (nseq, L), 1)
    first = _one_hot(sr == sb * TS)
    member = _one_hot((sr >> TS_SHIFT) == sb)
    return rowi, coli, same, first, member


def _mlstm_sample_body(*refs, NB):
    c_ref = refs[-3]

    @pl.when(pl.program_id(0) == 0)
    def _():
        _mlstm_sample_compute(*refs, NB=NB)

    @pl.when(pl.program_id(0) != 0)
    def _():
        c_ref[...] = jnp.zeros_like(c_ref)


def _mlstm_sample_compute(q_ref, k_ref, v_ref, g_ref, bias_ref, gn_ref, c0_ref, n0_ref, nrep_ref, mrep_ref, *rest, NB):
    y_ref, c_ref, n_ref, m_ref = rest[-4:]
    L = NB * TS
    rowi, coli, same, first, member = _seq_masks(L, NB)
    causal = jnp.logical_and(same, rowi >= coli)
    gates = g_ref[...] + bias_ref[...]
    lf = _log_sigmoid(gates)
    b_cols = _sel_left(_one_hot(causal), lf)
    btot_cols = _sel_left(_one_hot(same), lf)
    gi_t = gates.T[0:SUBLANES]
    lf_t = lf.T[0:SUBLANES]
    b_rows = _sel_right(lf_t, _one_hot(jnp.logical_and(same, rowi <= coli)))
    gnorm = gn_ref[...]
    q_all = q_ref[...].astype(F32)
    k_all = k_ref[...].astype(F32)
    v_all = v_ref[...].astype(F32)
    m_rep = mrep_ref[...]
    lane = lax.broadcasted_iota(jnp.int32, (L, LANES), 1)
    upper = lax.broadcasted_iota(jnp.int32, (WIN, 1), 0) < TS
    scale = DK ** -0.5
    for h in range(HEADS):
        qf = q_all[:, h * DK:(h + 1) * DK]
        kf = k_all[:, h * DK:(h + 1) * DK]
        vf = v_all[:, h * DV:(h + 1) * DV]
        qh, kh, vh = qf.astype(BF16), kf.astype(BF16), vf.astype(BF16)
        i_col = gates[:, h:h + 1]
        b_col = b_cols[:, HEADS + h:HEADS + h + 1]
        c_row = gi_t[h:h + 1, :] - b_rows[HEADS + h:HEADS + h + 1, :]
        m_prev = m_rep[:, h:h + 1]
        dm = jnp.where(causal, c_row, NEG)
        big_m = jnp.maximum(m_prev, jnp.max(dm, axis=-1, keepdims=True))
        m_last = jnp.maximum(m_prev, jnp.max(jnp.where(same, c_row, NEG), axis=-1, keepdims=True))
        pmat = jnp.exp(dm - big_m)
        w = (_nt(qh, kh) * scale) * pmat
        s_inter = jnp.exp(m_prev - big_m)
        qc = []
        for wd in range(NB // 2):
            qw = qf[WIN * wd:WIN * (wd + 1)].astype(BF16)
            o0 = _nt(qw, c0_ref[0, 2 * wd, h].astype(BF16))
            o1 = _nt(qw, c0_ref[0, 2 * wd + 1, h].astype(BF16))
            qc.append(jnp.where(upper, o0, o1))
        num = _nn(w.astype(BF16), vh) + s_inter * jnp.concatenate(qc, axis=0)
        qn = jnp.sum(qf * nrep_ref[:, h * DK:(h + 1) * DK], axis=-1, keepdims=True)
        den = jnp.sum(w, axis=-1, keepdims=True) + s_inter * qn
        hh = num / jnp.maximum(jnp.abs(den), jnp.exp(-(b_col + big_m)))
        y_ref[:, h * DV:(h + 1) * DV] = _head_layernorm(hh, gnorm[:, h * DV:(h + 1) * DV]).astype(BF16)
        wk = jnp.exp((i_col - b_col) - m_last)
        decay_col = jnp.exp(m_prev - m_last)
        kw = kf * (wk * scale)
        m_new_col = btot_cols[:, HEADS + h:HEADS + h + 1] + m_last
        z = jnp.where(lane == 0, decay_col, jnp.where(lane == 1, m_new_col, 0.0))
        zs = _sel_left(first, z)
        n_ref[h] = zs[:, 0:1] * n0_ref[h] + _sel_left(member, kw)
        m_ref[h] = jnp.broadcast_to(zs[:, 1:2], (NB, LANES))
        for b in range(NB):
            wd, par = b // 2, b % 2
            valid = upper if par == 0 else jnp.logical_not(upper)
            vwin = jnp.where(valid, vf[WIN * wd:WIN * (wd + 1)], 0.0).astype(BF16)
            kwin = kw[WIN * wd:WIN * (wd + 1)].astype(BF16)
            c_ref[0, b, h] = zs[b:b + 1, 0:1] * c0_ref[0, b, h] + _tn(vwin, kwin)


def _sample_grid(n_seq, depth, layer, has_prev):
    nblk = n_seq // SAMPLE_NB
    slots = 1 if has_prev else depth
    blk = lambda s, i: jnp.where(s == 0, i, nblk - 1)
    lay = lambda s: (layer + s) % depth
    return (slots, nblk), blk, lay


def _mlstm_sample(p, gates, bias, gnorm, c0, n0_hm, n_rep, m_rep, c_prev, *, layer, n_seq):
    nbk = SAMPLE_NB
    rb = nbk * TS
    has_prev = c_prev is not None
    grid, blk, lay = _sample_grid(n_seq, c0.shape[0], layer, has_prev)
    rows = lambda width, col: pl.BlockSpec((rb, width), lambda s, i, col=col: (blk(s, i), col))
    vec = lambda width: pl.BlockSpec((1, width), lambda s, i: (0, 0))
    hm = pl.BlockSpec((HEADS, nbk, LANES), lambda s, i: (0, blk(s, i), 0))
    in_specs = [rows(512, C512_MQ), rows(512, C512_MK), rows(1024, C1K_MV), rows(LANES, 0), vec(LANES), vec(D_MODEL),
                pl.BlockSpec((1, nbk, HEADS, DV, DK), lambda s, i: (layer, blk(s, i), 0, 0, 0)),
                hm, rows(HEADS * DK, 0), rows(LANES, 0)]
    args = [p, p, p, gates, bias, gnorm, c0, n0_hm, n_rep, m_rep]
    aliases = {}
    if has_prev:
        in_specs.append(pl.BlockSpec(memory_space=pl.ANY))
        args.append(c_prev)
        aliases[len(args) - 1] = 1
    return pl.pallas_call(
        functools.partial(_mlstm_sample_body, NB=nbk),
        grid=grid,
        in_specs=in_specs,
        out_specs=(rows(D_MODEL, 0),
                   pl.BlockSpec((1, nbk, HEADS, DV, DK), lambda s, i: (lay(s), i, 0, 0, 0)),
                   hm, hm),
        out_shape=(jax.ShapeDtypeStruct((n_seq * TS, D_MODEL), BF16),
                   jax.ShapeDtypeStruct(c0.shape, F32),
                   jax.ShapeDtypeStruct((HEADS, n_seq, LANES), F32),
                   jax.ShapeDtypeStruct((HEADS, n_seq, LANES), F32)),
        input_output_aliases=aliases,
        compiler_params=_cparams(2, 56), name="mlstm_sample",
    )(*args)


def _ref_rows(b, m, L):
    if m >= SUBLANES:
        nb = L // m
        b3 = b.reshape(nb, m, LANES)
        r = b3[:, m // 2 - 1:m // 2, :]
        return jnp.broadcast_to(r, (nb, m, LANES)).reshape(L, LANES)
    bt = b.reshape(L // SUBLANES, SUBLANES, LANES)
    sub = lax.broadcasted_iota(jnp.int32, bt.shape, 1)
    out = None
    for j in range(SUBLANES // m):
        idx = j * m + m // 2 - 1
        rj = jnp.broadcast_to(bt[:, idx:idx + 1, :], bt.shape)
        out = rj if out is None else jnp.where(sub >= j * m, rj, out)
    return out.reshape(L, LANES)


def _pair_level(L):
    rowi = lax.broadcasted_iota(jnp.int32, (L, L), 0)
    coli = lax.broadcasted_iota(jnp.int32, (L, L), 1)
    return jnp.where(rowi > coli, 31 - lax.clz(rowi ^ coli), -1)


def _gla_scores(qf, kf, b, L, top, level):
    a_mat = jnp.zeros((L, L), F32)
    m = top
    while m >= 2:
        d = b - _ref_rows(b, m, L)
        f = jnp.exp(jnp.minimum(d, -d))
        am = _nt((qf * f).astype(BF16), (kf * f).astype(BF16))
        a_mat = jnp.where(level == (m // 2).bit_length() - 1, am, a_mat)
        m //= 2
    return a_mat


def _gla_chunk(q, k, v, la, states, gnorm, L):
    rowi = lax.broadcasted_iota(jnp.int32, (L, L), 0)
    coli = lax.broadcasted_iota(jnp.int32, (L, L), 1)
    tril = _one_hot(rowi >= coli)
    level = _pair_level(L)
    scale = DK ** -0.5
    ys, new_states = [], []
    for h in range(HEADS):
        s_mat = states[h]
        qf = q[:, h * DK:(h + 1) * DK].astype(F32) * scale
        kf = k[:, h * DK:(h + 1) * DK].astype(F32)
        vh = v[:, h * DV:(h + 1) * DV]
        b = _sel_left(tril, la[:, h * DK:(h + 1) * DK])
        a_mat = _gla_scores(qf, kf, b, L, L, level)
        diag = jnp.sum(qf * kf, axis=-1, keepdims=True)
        o = (_nn(a_mat.astype(BF16), vh) + diag * vh.astype(F32)
             + _nn((qf * jnp.exp(b)).astype(BF16), s_mat.astype(BF16)))
        ys.append(_head_rmsnorm(o, gnorm[:, h * DV:(h + 1) * DV]))
        b_last = b[L - 1:L, :]
        kd = (kf * jnp.exp(b_last - b)).astype(BF16)
        e_col = jnp.broadcast_to(jnp.exp(b_last), (SUBLANES, LANES)).T[:, 0:1]
        new_states.append(e_col * s_mat + _tn(kd, vh))
    return ys, new_states


def _gla_log_decay(gates, w2_ref, b2_ref):
    return _log_sigmoid(_nn(gates.astype(BF16), w2_ref[...]) + b2_ref[...]) * (1.0 / GLA_TAU)


def _gla_prompt_body(q_ref, k_ref, v_ref, g_ref, w2_ref, b2_ref, gn_ref, y_ref, s_ref, *, L, NC):
    @pl.when(pl.program_id(1) == 0)
    def _():
        s_ref[...] = jnp.zeros_like(s_ref)

    la_all = _gla_log_decay(g_ref[...], w2_ref, b2_ref)
    gnorm = gn_ref[...]
    for c in range(NC):
        sl = slice(c * L, (c + 1) * L)
        states = [s_ref[0, h] for h in range(HEADS)]
        ys, new = _gla_chunk(q_ref[sl, :], k_ref[sl, :], v_ref[sl, :], la_all[sl], states, gnorm, L)
        for h in range(HEADS):
            y_ref[sl, h * DV:(h + 1) * DV] = ys[h].astype(BF16)
            s_ref[0, h] = new[h]


def _gla_prompt(p, gates, w2, b2, gnorm, *, nb, t_len):
    tb = MIX_TB
    nblk = t_len // tb
    rows = lambda width, col: pl.BlockSpec((tb, width), lambda b, t, col=col: (b * nblk + t, col))
    vec = lambda width: pl.BlockSpec((1, width), lambda b, t: (0, 0))
    return pl.pallas_call(
        functools.partial(_gla_prompt_body, L=GLA_CHUNK, NC=tb // GLA_CHUNK),
        grid=(nb, nblk),
        in_specs=[rows(512, C512_GQ), rows(512, C512_GK), rows(1024, C1K_GV), rows(LANES, 0),
                  pl.BlockSpec((LANES, 512), lambda b, t: (0, 0)), vec(512), vec(D_MODEL)],
        out_specs=(rows(D_MODEL, 0), pl.BlockSpec((1, HEADS, DK, DV), lambda b, t: (b, 0, 0, 0))),
        out_shape=(jax.ShapeDtypeStruct((nb * t_len, D_MODEL), BF16), jax.ShapeDtypeStruct((nb, HEADS, DK, DV), F32)),
        compiler_params=_cparams(2, 40), name="gla_prompt",
    )(p, p, p, gates, w2, b2, gnorm)


def _gla_sample_body(*refs, NB):
    s_ref = refs[-1]

    @pl.when(pl.program_id(0) == 0)
    def _():
        _gla_sample_compute(*refs, NB=NB)

    @pl.when(pl.program_id(0) != 0)
    def _():
        s_ref[...] = jnp.zeros_like(s_ref)


def _gla_sample_compute(q_ref, k_ref, v_ref, g_ref, w2_ref, b2_ref, gn_ref, s0_ref, *rest, NB):
    y_ref, s_ref = rest[-2:]
    L = NB * TS
    rowi, coli, same, first, _ = _seq_masks(L, NB)
    tril_seq = _one_hot(jnp.logical_and(same, rowi >= coli))
    full_seq = _one_hot(same)
    level = _pair_level(L)
    la_all = _gla_log_decay(g_ref[...], w2_ref, b2_ref)
    gnorm = gn_ref[...]
    q_all = q_ref[...].astype(F32)
    k_all = k_ref[...].astype(F32)
    v_all = v_ref[...].astype(F32)
    upper = lax.broadcasted_iota(jnp.int32, (WIN, 1), 0) < TS
    scale = DK ** -0.5
    for h in range(HEADS):
        qf = q_all[:, h * DK:(h + 1) * DK] * scale
        kf = k_all[:, h * DK:(h + 1) * DK]
        vf = v_all[:, h * DV:(h + 1) * DV]
        vh = vf.astype(BF16)
        la = la_all[:, h * DK:(h + 1) * DK]
        b = _sel_left(tril_seq, la)
        btot = _sel_left(full_seq, la)
        a_mat = _gla_scores(qf, kf, b, L, TS, level)
        diag = jnp.sum(qf * kf, axis=-1, keepdims=True)
        qe = qf * jnp.exp(b)
        inter = []
        for wd in range(NB // 2):
            qw = qe[WIN * wd:WIN * (wd + 1)].astype(BF16)
            o0 = _nn(qw, s0_ref[0, 2 * wd, h].astype(BF16))
            o1 = _nn(qw, s0_ref[0, 2 * wd + 1, h].astype(BF16))
            inter.append(jnp.where(upper, o0, o1))
        o = _nn(a_mat.astype(BF16), vh) + diag * vf + jnp.concatenate(inter, axis=0)
        y_ref[:, h * DV:(h + 1) * DV] = _head_rmsnorm(o, gnorm[:, h * DV:(h + 1) * DV]).astype(BF16)
        kd = kf * jnp.exp(btot - b)
        e_cols = jnp.exp(_sel_left(first, btot)).T
        for bq in range(NB):
            wd, par = bq // 2, bq % 2
            valid = upper if par == 0 else jnp.logical_not(upper)
            kwin = jnp.where(valid, kd[WIN * wd:WIN * (wd + 1)], 0.0).astype(BF16)
            vwin = vf[WIN * wd:WIN * (wd + 1)].astype(BF16)
            s_ref[0, bq, h] = e_cols[:, bq:bq + 1] * s0_ref[0, bq, h] + _tn(kwin, vwin)


def _gla_sample(p, gates, w2, b2, gnorm, s0, s_prev, *, layer, n_seq):
    nbk = SAMPLE_NB
    rb = nbk * TS
    has_prev = s_prev is not None
    grid, blk, lay = _sample_grid(n_seq, s0.shape[0], layer, has_prev)
    rows = lambda width, col: pl.BlockSpec((rb, width), lambda s, i, col=col: (blk(s, i), col))
    vec = lambda width: pl.BlockSpec((1, width), lambda s, i: (0, 0))
    in_specs = [rows(512, C512_GQ), rows(512, C512_GK), rows(1024, C1K_GV), rows(LANES, 0),
                pl.BlockSpec((LANES, 512), lambda s, i: (0, 0)), vec(512), vec(D_MODEL),
                pl.BlockSpec((1, nbk, HEADS, DK, DV), lambda s, i: (layer, blk(s, i), 0, 0, 0))]
    args = [p, p, p, gates, w2, b2, gnorm, s0]
    aliases = {}
    if has_prev:
        in_specs.append(pl.BlockSpec(memory_space=pl.ANY))
        args.append(s_prev)
        aliases[len(args) - 1] = 1
    return pl.pallas_call(
        functools.partial(_gla_sample_body, NB=nbk),
        grid=grid,
        in_specs=in_specs,
        out_specs=(rows(D_MODEL, 0), pl.BlockSpec((1, nbk, HEADS, DK, DV), lambda s, i: (lay(s), i, 0, 0, 0))),
        out_shape=(jax.ShapeDtypeStruct((n_seq * TS, D_MODEL), BF16), jax.ShapeDtypeStruct(s0.shape, F32)),
        input_output_aliases=aliases,
        compiler_params=_cparams(2, 56), name="gla_sample",
    )(*args)


def _lru_gates(xc, wa_ref, ba_ref, wi_ref, bi_ref, lam_ref):
    ra, ri = [], []
    for nb in range(LRU_BLOCKS):
        xs = xc[:, nb * LRU_BW:(nb + 1) * LRU_BW].astype(BF16)
        ra.append(_nn(xs, wa_ref[nb]))
        ri.append(_nn(xs, wi_ref[nb]))
    r = _sigmoid(jnp.concatenate(ra, axis=-1) + ba_ref[...])
    ig = _sigmoid(jnp.concatenate(ri, axis=-1) + bi_ref[...])
    a = jnp.exp(((-LRU_C) * _softplus(-lam_ref[...])) * r)
    om = 1.0 - a * a
    mult = om * lax.rsqrt(jnp.maximum(om, 1e-30))
    return a, ig * xc, mult


def _conv_shift_matrices(tb):
    nsh = CONV_W - 1
    j = jnp.arange(nsh)[:, None, None] + 1
    t = jnp.arange(tb)[None, :, None]
    s = jnp.arange(tb)[None, None, :]
    shift = (t - s == j).astype(BF16).reshape(nsh * tb, tb)
    t8 = jnp.arange(SUBLANES)[None, :, None]
    s8 = jnp.arange(SUBLANES)[None, None, :]
    tail = (s8 - t8 == SUBLANES - j).astype(BF16).reshape(nsh * SUBLANES, SUBLANES)
    return shift, tail


def _lru_prompt_body(x_ref, shift_ref, tail_ref, wc_ref, bc_ref, wa_ref, ba_ref, wi_ref, bi_ref, lam_ref,
                     h_ref, hfin_ref, cfin_ref, xprev, hcar, *, TB):
    t = pl.program_id(1)

    @pl.when(t == 0)
    def _():
        xprev[...] = jnp.zeros_like(xprev)
        hcar[...] = jnp.zeros_like(hcar)

    xb = x_ref[...]
    x = xb.astype(F32)
    nsh = CONV_W - 1
    shifted = _nn(shift_ref[...], xb)
    from_prev = _nn(tail_ref[...], xprev[...].astype(BF16))
    wc = wc_ref[...]
    xc = bc_ref[...] + wc[CONV_W - 1:CONV_W, :] * x
    for j in range(nsh):
        sj = shifted[j * TB:(j + 1) * TB]
        sj = jnp.concatenate([sj[0:SUBLANES] + from_prev[j * SUBLANES:(j + 1) * SUBLANES], sj[SUBLANES:]], axis=0)
        xc = xc + wc[CONV_W - 2 - j:CONV_W - 1 - j, :] * sj
    xprev[...] = x[TB - SUBLANES:TB, :]

    a, gx, mult = _lru_gates(xc, wa_ref, ba_ref, wi_ref, bi_ref, lam_ref)
    row8 = lax.broadcasted_iota(jnp.int32, (SUBLANES, 1), 0)
    m_first = jnp.where(row8 + t == 0, 1.0, mult[0:SUBLANES])
    u = jnp.concatenate([m_first, mult[SUBLANES:]], axis=0) * gx
    nt8 = TB // SUBLANES
    a3 = a.reshape(nt8, SUBLANES, D_MODEL)
    u3 = u.reshape(nt8, SUBLANES, D_MODEL)
    sub = lax.broadcasted_iota(jnp.int32, (1, SUBLANES, D_MODEL), 1)
    d = 1
    while d < SUBLANES:
        keep = sub >= d
        a_sh = jnp.where(keep, pltpu.roll(a3, d, 1), 1.0)
        u_sh = jnp.where(keep, pltpu.roll(u3, d, 1), 0.0)
        u3 = a3 * u_sh + u3
        a3 = a3 * a_sh
        d *= 2
    h_in = hcar[...]
    hs = []
    for j in range(nt8):
        hj = a3[j] * h_in + u3[j]
        hs.append(hj)
        h_in = hj[SUBLANES - 1:SUBLANES]
    h_ref[...] = jnp.concatenate(hs, axis=0).astype(BF16)
    hcar[...] = h_in
    hfin_ref[0] = h_in
    cfin_ref[0] = x[TB - (CONV_W - 1):TB, :]


def _lru_prompt(p, wc, bc, wa, ba, wi, bi, lam, *, nb, t_len):
    tb = LRU_TB
    nblk = t_len // tb
    vec = pl.BlockSpec((1, D_MODEL), lambda b, t: (0, 0))
    wblk = pl.BlockSpec((LRU_BLOCKS, LRU_BW, LRU_BW), lambda b, t: (0, 0, 0))
    shift, tail = _conv_shift_matrices(tb)
    whole = lambda arr: pl.BlockSpec(arr.shape, lambda b, t: (0, 0))
    return pl.pallas_call(
        functools.partial(_lru_prompt_body, TB=tb),
        grid=(nb, nblk),
        in_specs=[pl.BlockSpec((tb, D_MODEL), lambda b, t: (b * nblk + t, C1K_LX)), whole(shift), whole(tail),
                  pl.BlockSpec((CONV_W, D_MODEL), lambda b, t: (0, 0)), vec, wblk, vec, wblk, vec, vec],
        out_specs=(pl.BlockSpec((tb, D_MODEL), lambda b, t: (b * nblk + t, 0)),
                   pl.BlockSpec((1, 1, D_MODEL), lambda b, t: (b, 0, 0)),
                   pl.BlockSpec((1, CONV_W - 1, D_MODEL), lambda b, t: (b, 0, 0))),
        out_shape=(jax.ShapeDtypeStruct((nb * t_len, D_MODEL), BF16),
                   jax.ShapeDtypeStruct((nb, 1, D_MODEL), F32),
                   jax.ShapeDtypeStruct((nb, CONV_W - 1, D_MODEL), F32)),
        scratch_shapes=[pltpu.VMEM((SUBLANES, D_MODEL), F32), pltpu.VMEM((1, D_MODEL), F32)],
        compiler_params=_cparams(2, 40), name="lru_prompt",
    )(p, shift, tail, wc, bc, wa, ba, wi, bi, lam)


def _lru_sample_body(x_ref, buf_ref, h0_ref, wc_ref, bc_ref, wa_ref, ba_ref, wi_ref, bi_ref, lam_ref,
                     h_ref, hfin_ref, cfin_ref, *, T):
    wc = wc_ref[...]
    xs = [buf_ref[j] for j in range(CONV_W - 1)] + [x_ref[t].astype(F32) for t in range(T)]
    h = h0_ref[...]
    for t in range(T):
        xc = bc_ref[...]
        for j in range(CONV_W):
            xc = xc + wc[j:j + 1, :] * xs[t + j]
        a, gx, mult = _lru_gates(xc, wa_ref, ba_ref, wi_ref, bi_ref, lam_ref)
        h = a * h + mult * gx
        h_ref[t] = h.astype(BF16)
    hfin_ref[...] = h
    for j in range(CONV_W - 1):
        cfin_ref[j] = xs[T + j]


def _lru_sample(x_tm, buf_tm, h0, wc, bc, wa, ba, wi, bi, lam):
    t_len, n_seq, _ = x_tm.shape
    return pl.pallas_call(
        functools.partial(_lru_sample_body, T=t_len),
        out_shape=(jax.ShapeDtypeStruct((t_len, n_seq, D_MODEL), BF16),
                   jax.ShapeDtypeStruct((n_seq, D_MODEL), F32),
                   jax.ShapeDtypeStruct((CONV_W - 1, n_seq, D_MODEL), F32)),
        compiler_params=pltpu.CompilerParams(vmem_limit_bytes=40 * 1024 * 1024), name="lru_sample",
    )(x_tm, buf_tm, h0, wc, bc, wa, ba, wi, bi, lam)


def _proj_weights(w):
    col = lambda r: w[:, r[0]:r[1]]
    w_big = jnp.concatenate([col(W_IN_A), col(W_IN_B), col(W_IN_C)], axis=1).astype(BF16)
    pad = jnp.zeros((D_MODEL, LANES - 2 * HEADS - GLA_RANK), F32)
    w_small = jnp.concatenate([col(W_IN_IF), col(W_IN_LR), pad], axis=1).astype(BF16)
    return w_big, w_small


def kernel(x_prompt, x_sample, state_mlstm_C, state_mlstm_n, state_mlstm_m, state_gla_S, state_lru_h, state_lru_conv,
           norm_ffn1, w_ffn1_in, w_ffn1_out, norm_mix, w_in, b_ml_if, g_ml_norm, w_gla_lr2, b_gla_gate, g_gla_norm,
           w_conv, b_conv, w_lru_a, b_lru_a, w_lru_i, b_lru_i, lru_lambda, w_out, norm_ffn2, w_ffn2_in, w_ffn2_out,
           norm_final):
    nb, t_len, d = x_prompt.shape
    n_seq, ts, _ = x_sample.shape
    depth = w_in.shape[0]
    assert d == D_MODEL and ts == TS and t_len % MIX_TB == 0 and n_seq % SAMPLE_NB == 0

    xp = x_prompt.reshape(nb * t_len, d)
    xs = x_sample.reshape(n_seq * ts, d)

    new_p = {k: [] for k in ("C", "n", "m", "S", "h", "conv")}
    new_s = {k: [] for k in ("n", "m", "h", "conv")}
    s_c_all, s_s_all = None, None
    yp = ys = None
    for l in range(depth):
        w_big, w_small = _proj_weights(w_in[l])
        bias_if = jnp.concatenate([b_ml_if[l], jnp.zeros((LANES - 2 * HEADS,), F32)]).reshape(1, LANES)
        w2 = jnp.zeros((LANES, HEADS * DK), F32).at[2 * HEADS:2 * HEADS + GLA_RANK].set(w_gla_lr2[l]).astype(BF16)
        b2 = b_gla_gate[l].reshape(1, HEADS * DK)
        g_ml = g_ml_norm[l].reshape(1, d)
        g_gla = g_gla_norm[l].reshape(1, d)
        wc, bc = w_conv[l], b_conv[l].reshape(1, d)
        wa, wi = w_lru_a[l].astype(BF16), w_lru_i[l].astype(BF16)
        ba, bi, lam = b_lru_a[l].reshape(1, d), b_lru_i[l].reshape(1, d), lru_lambda[l].reshape(1, d)
        f1_in, f1_out = w_ffn1_in[l].astype(BF16), w_ffn1_out[l].astype(BF16)
        f2_in, f2_out = w_ffn2_in[l].astype(BF16), w_ffn2_out[l].astype(BF16)
        w_o = w_out[l].astype(BF16)
        last = l == depth - 1
        g_next = norm_final if last else norm_ffn1[l + 1]

        x1, xn = _ffn(xp, norm_ffn1[l], f1_in, f1_out, norm_mix[l], final=False)
        p, gates = _proj(xn, w_big, w_small)
        g_ml_rep = jnp.broadcast_to(g_ml_norm[l][:, None], (d, LANES))
        y_ml, p_c, p_n, p_m = _mlstm_prompt(p, gates, bias_if, g_ml_rep, nb=nb, t_len=t_len)
        y_gla, p_s = _gla_prompt(p, gates, w2, b2, g_gla, nb=nb, t_len=t_len)
        h_lru, p_h, p_conv = _lru_prompt(p, wc, bc, wa, ba, wi, bi, lam, nb=nb, t_len=t_len)
        x2 = _merge(x1, p, y_ml, y_gla, h_lru, w_o)
        if last:
            yp = _ffn(x2, norm_ffn2[l], f2_in, f2_out, g_next, final=True)
        else:
            xp, _ = _ffn(x2, norm_ffn2[l], f2_in, f2_out, g_next, final=False)

        x1, xn = _ffn(xs, norm_ffn1[l], f1_in, f1_out, norm_mix[l], final=False)
        p, gates = _proj(xn, w_big, w_small)
        n0_hm = state_mlstm_n[l].transpose(1, 0, 2)
        n_rep = jnp.repeat(state_mlstm_n[l].reshape(n_seq, HEADS * DK), ts, axis=0)
        m_rep = jnp.repeat(jnp.pad(state_mlstm_m[l], ((0, 0), (0, LANES - HEADS))), ts, axis=0)
        y_ml, s_c_all, s_n, s_m = _mlstm_sample(p, gates, bias_if, g_ml, state_mlstm_C, n0_hm, n_rep, m_rep,
                                                s_c_all, layer=l, n_seq=n_seq)
        y_gla, s_s_all = _gla_sample(p, gates, w2, b2, g_gla, state_gla_S, s_s_all, layer=l, n_seq=n_seq)
        lx_s = p[:, C1K_LX * 1024:(C1K_LX + 1) * 1024].reshape(n_seq, ts, d).transpose(1, 0, 2)
        hs_tm, s_h, s_conv_tm = _lru_sample(lx_s, state_lru_conv[l].transpose(1, 0, 2), state_lru_h[l],
                                            wc, bc, wa, ba, wi, bi, lam)
        h_lru = hs_tm.transpose(1, 0, 2).reshape(n_seq * ts, d)
        x2 = _merge(x1, p, y_ml, y_gla, h_lru, w_o)
        if last:
            ys = _ffn(x2, norm_ffn2[l], f2_in, f2_out, g_next, final=True)
        else:
            xs, _ = _ffn(x2, norm_ffn2[l], f2_in, f2_out, g_next, final=False)

        new_p["C"].append(p_c)
        new_p["n"].append(p_n)
        new_p["m"].append(p_m[:, :, 0])
        new_p["S"].append(p_s)
        new_p["h"].append(p_h[:, 0, :])
        new_p["conv"].append(p_conv)
        new_s["n"].append(s_n.transpose(1, 0, 2))
        new_s["m"].append(s_m[:, :, 0].T)
        new_s["h"].append(s_h)
        new_s["conv"].append(s_conv_tm.transpose(1, 0, 2))

    st = lambda arrs: jnp.stack(arrs)
    return (yp.reshape(nb, t_len, d), ys.reshape(n_seq, ts, d),
            st(new_p["C"]), st(new_p["n"]), st(new_p["m"]), st(new_p["S"]), st(new_p["h"]), st(new_p["conv"]),
            s_c_all, st(new_s["n"]), st(new_s["m"]), s_s_all, st(new_s["h"]), st(new_s["conv"]))
```

```python
import functools

import jax
import jax.numpy as jnp
from jax import lax
from jax.experimental import pallas as pl
from jax.experimental.pallas import tpu as pltpu

F32 = jnp.float32
BF16 = jnp.bfloat16

D_MODEL = 1024
D_FF = 2816
HEADS = 4
DK = 128
DV = 256
GLA_RANK = 16
GLA_TAU = 16.0
LRU_BLOCKS = 8
LRU_BW = 128
LRU_C = 8.0
CONV_W = 4
EPS = 1e-6
NEG = -1e30

LANES = 128
SUBLANES = 8
MXU_N = 256

P_COLS = 11264
C512_MQ, C512_MK, C512_GQ, C512_GK = 0, 1, 6, 7
C1K_MV, C1K_MO, C1K_GV, C1K_GG, C1K_LX, C1K_LY, C1K_MG0 = 1, 2, 4, 5, 6, 7, 8
W_IN_A = (0, 3072)
W_IN_IF = (3072, 3080)
W_IN_B = (3080, 6152)
W_IN_LR = (6152, 6168)
W_IN_C = (6168, 11288)
PROJ_TN = 1024
PROJ_WIN = PROJ_TN // 128 + 1
PROJ_REGIONS = ((0, 3, 0), (3, 6, W_IN_B[0] - W_IN_A[1]), (6, 11, W_IN_C[0] - W_IN_B[1] + W_IN_B[0] - W_IN_A[1]))

ML_CHUNK = 256
GLA_CHUNK = 128
MIX_TB = 512
LRU_TB = 256
TS = 4
TS_SHIFT = 2
SAMPLE_NB = 16
WIN = 8
FF_CHUNK = MXU_N


def _cparams(n_axes, vmem_mib):
    return pltpu.CompilerParams(dimension_semantics=("arbitrary",) * n_axes,
                                vmem_limit_bytes=vmem_mib * 1024 * 1024)


def _pick_tile(n, target, mult):
    best = None
    for t in range(mult, min(n, target) + 1, mult):
        if n % t == 0:
            best = t
    assert best is not None, (n, target, mult)
    return best


def _sigmoid(x):
    return 0.5 * jnp.tanh(0.5 * x) + 0.5


def _log_sigmoid(x):
    return jnp.minimum(x, 0.0) - jnp.log(1.0 + jnp.exp(-jnp.abs(x)))


def _softplus(x):
    return jnp.maximum(x, 0.0) + jnp.log(1.0 + jnp.exp(-jnp.abs(x)))


def _rms(x, g):
    return x * lax.rsqrt(jnp.mean(x * x, axis=-1, keepdims=True) + EPS) * g


def _nn(a, b):
    return jnp.dot(a, b, preferred_element_type=F32)


def _nt(a, b):
    return lax.dot_general(a, b, (((1,), (1,)), ((), ())), preferred_element_type=F32)


def _tn(a, b):
    return lax.dot_general(a, b, (((0,), (0,)), ((), ())), preferred_element_type=F32)


def _split3(x):
    hi = x.astype(BF16)
    r1 = x - hi.astype(F32)
    mid = r1.astype(BF16)
    lo = (r1 - mid.astype(F32)).astype(BF16)
    return hi, mid, lo


def _sel_left(sel, x):
    hi, mid, lo = _split3(x)
    return _nn(sel, hi) + _nn(sel, mid) + _nn(sel, lo)


def _sel_right(x, sel):
    hi, mid, lo = _split3(x)
    return _nn(hi, sel) + _nn(mid, sel) + _nn(lo, sel)


def _one_hot(mask):
    return jnp.where(mask, 1.0, 0.0).astype(BF16)


def _head_layernorm(h, g):
    mu = jnp.mean(h, axis=-1, keepdims=True)
    hc = h - mu
    return hc * lax.rsqrt(jnp.mean(hc * hc, axis=-1, keepdims=True) + EPS) * g


def _head_rmsnorm(h, g):
    return h * lax.rsqrt(jnp.mean(h * h, axis=-1, keepdims=True) + EPS) * g


def _ffn_body(x_ref, g_ref, win_ref, wout_ref, gn_ref, *rest, n_chunks, final, with_gates):
    if with_gates:
        ws_ref, rest = rest[0], rest[1:]
    if final:
        y_ref, xn_s, acc_s = rest
    elif with_gates:
        y_ref, xn_ref, gates_ref, xn_s, acc_s = rest
    else:
        y_ref, xn_ref, xn_s, acc_s = rest
    xn_s[...] = _rms(x_ref[...], g_ref[...]).astype(BF16)
    for c in range(n_chunks):
        xn = xn_s[...]
        g = _nn(xn, win_ref[:, c * FF_CHUNK:(c + 1) * FF_CHUNK])
        u = _nn(xn, win_ref[:, D_FF + c * FF_CHUNK:D_FF + (c + 1) * FF_CHUNK])
        t = jnp.tanh(0.5 * g)
        a = (g * (t + 1.0)) * u
        part = _nn(a.astype(BF16), wout_ref[c * FF_CHUNK:(c + 1) * FF_CHUNK, :])
        if c == 0:
            acc_s[...] = part
        else:
            acc_s[...] += part
    xo = x_ref[...] + 0.25 * acc_s[...]
    if final:
        y_ref[...] = _rms(xo, gn_ref[...])
    else:
        y_ref[...] = xo
        xn_next = _rms(xo, gn_ref[...]).astype(BF16)
        xn_ref[...] = xn_next
        if with_gates:
            gates_ref[...] = _nn(xn_next, ws_ref[...])


def _ffn(x, g, w_in_c, w_out_c, g_next, *, final, w_small=None):
    rows = x.shape[0]
    tm = _pick_tile(rows, 1024, 16)
    n_chunks = D_FF // FF_CHUNK
    with_gates = w_small is not None
    resident = lambda shape: pl.BlockSpec(shape, lambda i: (0,) * len(shape), pipeline_mode=pl.Buffered(1))
    row_spec = pl.BlockSpec((tm, D_MODEL), lambda i: (i, 0))
    vec = pl.BlockSpec((1, D_MODEL), lambda i: (0, 0))
    in_specs = [row_spec, vec, resident(w_in_c.shape), resident(w_out_c.shape), vec]
    args = [x, g.reshape(1, D_MODEL), w_in_c, w_out_c, g_next.reshape(1, D_MODEL)]
    if final:
        out_shape = jax.ShapeDtypeStruct((rows, D_MODEL), F32)
        out_specs = row_spec
    else:
        out_shape = [jax.ShapeDtypeStruct((rows, D_MODEL), F32), jax.ShapeDtypeStruct((rows, D_MODEL), BF16)]
        out_specs = [row_spec, row_spec]
        if with_gates:
            in_specs.append(resident(w_small.shape))
            args.append(w_small)
            out_shape.append(jax.ShapeDtypeStruct((rows, LANES), F32))
            out_specs.append(pl.BlockSpec((tm, LANES), lambda i: (i, 0)))
    return pl.pallas_call(
        functools.partial(_ffn_body, n_chunks=n_chunks, final=final, with_gates=with_gates),
        grid=(rows // tm,),
        in_specs=in_specs, out_specs=out_specs, out_shape=out_shape,
        scratch_shapes=[pltpu.VMEM((tm, D_MODEL), BF16), pltpu.VMEM((tm, D_MODEL), F32)],
        compiler_params=_cparams(1, 56), name="ffn_final" if final else "ffn",
    )(*args)


def _proj_body(xn_ref, *rest):
    w_refs, p_ref, w_s = rest[:PROJ_WIN], rest[PROJ_WIN], rest[PROJ_WIN + 1]
    j = pl.program_id(0)

    def build(shift):
        cat = jnp.concatenate([r[...] for r in w_refs], axis=1)
        w_s[...] = cat[:, shift:shift + PROJ_TN].astype(BF16)

    first_row_tile = pl.program_id(1) == 0
    for lo, hi, shift in PROJ_REGIONS:
        pl.when(jnp.logical_and(first_row_tile, jnp.logical_and(j >= lo, j < hi)))(functools.partial(build, shift))
    p_ref[...] = _nn(xn_ref[...], w_s[...]).astype(BF16)


def _proj(xn, w_in_all, layer):
    rows = xn.shape[0]
    tm = _pick_tile(rows, 2048, 16)
    wblk = lambda k: pl.BlockSpec((None, D_MODEL, LANES), lambda j, i, k=k: (layer, 0, j * (PROJ_TN // LANES) + k))
    return pl.pallas_call(
        _proj_body, grid=(P_COLS // PROJ_TN, rows // tm),
        in_specs=[pl.BlockSpec((tm, D_MODEL), lambda j, i: (i, 0))] + [wblk(k) for k in range(PROJ_WIN)],
        out_specs=pl.BlockSpec((tm, PROJ_TN), lambda j, i: (i, j)),
        out_shape=jax.ShapeDtypeStruct((rows, P_COLS), BF16),
        scratch_shapes=[pltpu.VMEM((D_MODEL, PROJ_TN), BF16)],
        compiler_params=_cparams(2, 48), name="proj",
    )(xn, *([w_in_all] * PROJ_WIN))


def _merge_body(x_ref, mo_ref, gg_ref, ly_ref, m0_ref, m1_ref, m2_ref, yml_ref, ygla_ref, hlru_ref, w_ref, o_ref):
    f = lambda r: r[...].astype(F32)
    th = lambda r: jnp.tanh(0.5 * f(r)) + 1.0
    gg = f(gg_ref)
    ly = f(ly_ref)
    y_ml = f(yml_ref) * th(mo_ref)
    y_gla = f(ygla_ref) * (gg * th(gg_ref))
    y_lru = f(hlru_ref) * (ly * (1.0 + jnp.tanh(0.7978845608028654 * (ly + 0.044715 * (ly * ly * ly)))))
    merged4 = th(m0_ref) * y_ml + th(m1_ref) * y_gla + th(m2_ref) * y_lru
    o_ref[...] = x_ref[...] + _nn((0.25 * merged4).astype(BF16), w_ref[...])


def _merge(x, p, y_ml, y_gla, h_lru, w_out):
    rows = x.shape[0]
    tm = _pick_tile(rows, 512, 16)
    row = lambda c: pl.BlockSpec((tm, D_MODEL), lambda i, c=c: (i, c))
    return pl.pallas_call(
        _merge_body, grid=(rows // tm,),
        in_specs=[row(0), row(C1K_MO), row(C1K_GG), row(C1K_LY), row(C1K_MG0), row(C1K_MG0 + 1), row(C1K_MG0 + 2),
                  row(0), row(0), row(0), pl.BlockSpec((D_MODEL, D_MODEL), lambda i: (0, 0))],
        out_specs=row(0),
        out_shape=jax.ShapeDtypeStruct((rows, D_MODEL), F32),
        compiler_params=_cparams(1, 56), name="merge",
    )(x, p, p, p, p, p, p, y_ml, y_gla, h_lru, w_out)


def _mlstm_chunk(q, k, v, gi, lf, states, g_rep, L):
    rowi = lax.broadcasted_iota(jnp.int32, (L, L), 0)
    coli = lax.broadcasted_iota(jnp.int32, (L, L), 1)
    upper = rowi <= coli
    b_cols = _sel_left(_one_hot(rowi >= coli), lf)
    lf_t = lf.T[0:SUBLANES]
    b_rows = _sel_right(lf_t, _one_hot(upper))
    scale = DK ** -0.5
    ys, new_states = [], []
    for h in range(HEADS):
        c_mat, n_vec, m_prev = states[h]
        qh = q[:, h * DK:(h + 1) * DK]
        kh = k[:, h * DK:(h + 1) * DK]
        vh = v[:, h * DV:(h + 1) * DV]
        b_col = b_cols[:, HEADS + h:HEADS + h + 1]
        b_row = b_rows[HEADS + h:HEADS + h + 1, :]
        c_col = gi[:, h:h + 1] - b_col
        dm = jnp.where(upper, c_col, NEG)
        big_m = jnp.maximum(m_prev, jnp.max(dm, axis=0, keepdims=True))
        w = (_nt(kh, qh) * scale) * jnp.exp(dm - big_m)
        s_inter = jnp.exp(m_prev - big_m)
        num = _tn(vh, w.astype(BF16)) + s_inter * _nt(c_mat.astype(BF16), qh)
        qn = _nt(jnp.broadcast_to(n_vec, (SUBLANES, DK)).astype(BF16), qh)[0:1, :]
        den = jnp.sum(w, axis=0, keepdims=True) + s_inter * qn
        hh = num / jnp.maximum(jnp.abs(den), jnp.exp(-(b_row + big_m)))
        mu = jnp.mean(hh, axis=0, keepdims=True)
        hc = hh - mu
        g_h = jnp.concatenate([g_rep[h * DV:(h + 1) * DV]] * (L // LANES), axis=1)
        y_t = hc * lax.rsqrt(jnp.mean(hc * hc, axis=0, keepdims=True) + EPS) * g_h
        ys.append(y_t.astype(BF16).T)
        m_last = big_m[:, L - 1:L]
        wk = jnp.exp(c_col - m_last)
        decay = jnp.exp(m_prev - m_last)
        kw = kh.astype(F32) * (wk * scale)
        c_new = decay * c_mat + _tn(vh, kw.astype(BF16))
        n_new = decay * n_vec + jnp.sum(kw, axis=0, keepdims=True)
        m_new = b_col[L - 1:L, :] + m_last
        new_states.append((c_new, n_new, m_new))
    return ys, new_states


def _mlstm_prompt_body(q_ref, k_ref, v_ref, g_ref, bias_ref, gn_ref, y_ref, c_ref, n_ref, m_ref, *, L, NC):
    @pl.when(pl.program_id(1) == 0)
    def _():
        c_ref[...] = jnp.zeros_like(c_ref)
        n_ref[...] = jnp.zeros_like(n_ref)
        m_ref[...] = jnp.zeros_like(m_ref)

    gates = g_ref[...] + bias_ref[...]
    lf_all = _log_sigmoid(gates)
    gnorm = gn_ref[...]
    for c in range(NC):
        sl = slice(c * L, (c + 1) * L)
        states = [(c_ref[0, h], n_ref[0, h:h + 1, :], m_ref[0, h:h + 1, 0:1]) for h in range(HEADS)]
        ys, new = _mlstm_chunk(q_ref[sl, :], k_ref[sl, :], v_ref[sl, :], gates[sl], lf_all[sl], states, gnorm, L)
        for h in range(HEADS):
            y_ref[sl, h * DV:(h + 1) * DV] = ys[h]
            c_ref[0, h] = new[h][0]
            n_ref[0, h:h + 1, :] = new[h][1]
            m_ref[0, h:h + 1, :] = jnp.broadcast_to(new[h][2], (1, LANES))


def _mlstm_prompt(p, gates, bias, g_rep, *, nb, t_len):
    tb = MIX_TB
    nblk = t_len // tb
    rows = lambda width, col: pl.BlockSpec((tb, width), lambda b, t, col=col: (b * nblk + t, col))
    vec = lambda width: pl.BlockSpec((1, width), lambda b, t: (0, 0))
    return pl.pallas_call(
        functools.partial(_mlstm_prompt_body, L=ML_CHUNK, NC=tb // ML_CHUNK),
        grid=(nb, nblk),
        in_specs=[rows(512, C512_MQ), rows(512, C512_MK), rows(1024, C1K_MV), rows(LANES, 0), vec(LANES),
                  pl.BlockSpec((D_MODEL, LANES), lambda b, t: (0, 0))],
        out_specs=(rows(D_MODEL, 0),
                   pl.BlockSpec((1, HEADS, DV, DK), lambda b, t: (b, 0, 0, 0)),
                   pl.BlockSpec((1, HEADS, DK), lambda b, t: (b, 0, 0)),
                   pl.BlockSpec((1, HEADS, LANES), lambda b, t: (b, 0, 0))),
        out_shape=(jax.ShapeDtypeStruct((nb * t_len, D_MODEL), BF16),
                   jax.ShapeDtypeStruct((nb, HEADS, DV, DK), F32),
                   jax.ShapeDtypeStruct((nb, HEADS, DK), F32),
                   jax.ShapeDtypeStruct((nb, HEADS, LANES), F32)),
        compiler_params=_cparams(2, 40), name="mlstm_prompt",
    )(p, p, p, gates, bias, g_rep)


def _seq_masks(L, nseq):
    rowi = lax.broadcasted_iota(jnp.int32, (L, L), 0)
    coli = lax.broadcasted_iota(jnp.int32, (L, L), 1)
    same = (rowi >> TS_SHIFT) == (coli >> TS_SHIFT)
    sb = lax.broadcasted_iota(jnp.int32, (nseq, L), 0)
    sr = lax.broadcasted_iota(jnp.int32, (nseq, L), 1)
    first = _one_hot(sr == sb * TS)
    member = _one_hot((sr >> TS_SHIFT) == sb)
    return rowi, coli, same, first, member


def _mlstm_sample_body(*refs, NB):
    c_ref = refs[-3]

    @pl.when(pl.program_id(0) == 0)
    def _():
        _mlstm_sample_compute(*refs, NB=NB)

    @pl.when(pl.program_id(0) != 0)
    def _():
        c_ref[...] = jnp.zeros_like(c_ref)


def _mlstm_sample_compute(q_ref, k_ref, v_ref, g_ref, bias_ref, gn_ref, c0_ref, n0_ref, nrep_ref, mrep_ref, *rest, NB):
    y_ref, c_ref, n_ref, m_ref = rest[-4:]
    L = NB * TS
    rowi, coli, same, first, member = _seq_masks(L, NB)
    causal = jnp.logical_and(same, rowi >= coli)
    gates = g_ref[...] + bias_ref[...]
    lf = _log_sigmoid(gates)
    b_cols = _sel_left(_one_hot(causal), lf)
    btot_cols = _sel_left(_one_hot(same), lf)
    gi_t = gates.T[0:SUBLANES]
    lf_t = lf.T[0:SUBLANES]
    b_rows = _sel_right(lf_t, _one_hot(jnp.logical_and(same, rowi <= coli)))
    gnorm = gn_ref[...]
    q_all = q_ref[...].astype(F32)
    k_all = k_ref[...].astype(F32)
    v_all = v_ref[...].astype(F32)
    m_rep = mrep_ref[...]
    lane = lax.broadcasted_iota(jnp.int32, (L, LANES), 1)
    upper = lax.broadcasted_iota(jnp.int32, (WIN, 1), 0) < TS
    scale = DK ** -0.5
    for h in range(HEADS):
        qf = q_all[:, h * DK:(h + 1) * DK]
        kf = k_all[:, h * DK:(h + 1) * DK]
        vf = v_all[:, h * DV:(h + 1) * DV]
        qh, kh, vh = qf.astype(BF16), kf.astype(BF16), vf.astype(BF16)
        i_col = gates[:, h:h + 1]
        b_col = b_cols[:, HEADS + h:HEADS + h + 1]
        c_row = gi_t[h:h + 1, :] - b_rows[HEADS + h:HEADS + h + 1, :]
        m_prev = m_rep[:, h:h + 1]
        dm = jnp.where(causal, c_row, NEG)
        big_m = jnp.maximum(m_prev, jnp.max(dm, axis=-1, keepdims=True))
        m_last = jnp.maximum(m_prev, jnp.max(jnp.where(same, c_row, NEG), axis=-1, keepdims=True))
        pmat = jnp.exp(dm - big_m)
        w = (_nt(qh, kh) * scale) * pmat
        s_inter = jnp.exp(m_prev - big_m)
        qc = []
        for wd in range(NB // 2):
            qw = qf[WIN * wd:WIN * (wd + 1)].astype(BF16)
            o0 = _nt(qw, c0_ref[0, 2 * wd, h].astype(BF16))
            o1 = _nt(qw, c0_ref[0, 2 * wd + 1, h].astype(BF16))
            qc.append(jnp.where(upper, o0, o1))
        num = _nn(w.astype(BF16), vh) + s_inter * jnp.concatenate(qc, axis=0)
        qn = jnp.sum(qf * nrep_ref[:, h * DK:(h + 1) * DK], axis=-1, keepdims=True)
        den = jnp.sum(w, axis=-1, keepdims=True) + s_inter * qn
        hh = num / jnp.maximum(jnp.abs(den), jnp.exp(-(b_col + big_m)))
        y_ref[:, h * DV:(h + 1) * DV] = _head_layernorm(hh, gnorm[:, h * DV:(h + 1) * DV]).astype(BF16)
        wk = jnp.exp((i_col - b_col) - m_last)
        decay_col = jnp.exp(m_prev - m_last)
        kw = kf * (wk * scale)
        m_new_col = btot_cols[:, HEADS + h:HEADS + h + 1] + m_last
        z = jnp.where(lane == 0, decay_col, jnp.where(lane == 1, m_new_col, 0.0))
        zs = _sel_left(first, z)
        n_ref[h] = zs[:, 0:1] * n0_ref[h] + _sel_left(member, kw)
        m_ref[h] = jnp.broadcast_to(zs[:, 1:2], (NB, LANES))
        for b in range(NB):
            wd, par = b // 2, b % 2
            valid = upper if par == 0 else jnp.logical_not(upper)
            vwin = jnp.where(valid, vf[WIN * wd:WIN * (wd + 1)], 0.0).astype(BF16)
            kwin = kw[WIN * wd:WIN * (wd + 1)].astype(BF16)
            c_ref[0, b, h] = zs[b:b + 1, 0:1] * c0_ref[0, b, h] + _tn(vwin, kwin)


def _sample_grid(n_seq, depth, layer, has_prev):
    nblk = n_seq // SAMPLE_NB
    slots = 1 if has_prev else depth
    blk = lambda s, i: jnp.where(s == 0, i, nblk - 1)
    lay = lambda s: (layer + s) % depth
    return (slots, nblk), blk, lay


def _mlstm_sample(p, gates, bias, gnorm, c0, n0_hm, n_rep, m_rep, c_prev, *, layer, n_seq):
    nbk = SAMPLE_NB
    rb = nbk * TS
    has_prev = c_prev is not None
    grid, blk, lay = _sample_grid(n_seq, c0.shape[0], layer, has_prev)
    rows = lambda width, col: pl.BlockSpec((rb, width), lambda s, i, col=col: (blk(s, i), col))
    vec = lambda width: pl.BlockSpec((1, width), lambda s, i: (0, 0))
    hm = pl.BlockSpec((HEADS, nbk, LANES), lambda s, i: (0, blk(s, i), 0))
    in_specs = [rows(512, C512_MQ), rows(512, C512_MK), rows(1024, C1K_MV), rows(LANES, 0), vec(LANES), vec(D_MODEL),
                pl.BlockSpec((1, nbk, HEADS, DV, DK), lambda s, i: (layer, blk(s, i), 0, 0, 0)),
                hm, rows(HEADS * DK, 0), rows(LANES, 0)]
    args = [p, p, p, gates, bias, gnorm, c0, n0_hm, n_rep, m_rep]
    aliases = {}
    if has_prev:
        in_specs.append(pl.BlockSpec(memory_space=pl.ANY))
        args.append(c_prev)
        aliases[len(args) - 1] = 1
    return pl.pallas_call(
        functools.partial(_mlstm_sample_body, NB=nbk),
        grid=grid,
        in_specs=in_specs,
        out_specs=(rows(D_MODEL, 0),
                   pl.BlockSpec((1, nbk, HEADS, DV, DK), lambda s, i: (lay(s), i, 0, 0, 0)),
                   hm, hm),
        out_shape=(jax.ShapeDtypeStruct((n_seq * TS, D_MODEL), BF16),
                   jax.ShapeDtypeStruct(c0.shape, F32),
                   jax.ShapeDtypeStruct((HEADS, n_seq, LANES), F32),
                   jax.ShapeDtypeStruct((HEADS, n_seq, LANES), F32)),
        input_output_aliases=aliases,
        compiler_params=_cparams(2, 56), name="mlstm_sample",
    )(*args)


def _ref_rows(b, m, L):
    if m >= SUBLANES:
        nb = L // m
        b3 = b.reshape(nb, m, LANES)
        r = b3[:, m // 2 - 1:m // 2, :]
        return jnp.broadcast_to(r, (nb, m, LANES)).reshape(L, LANES)
    bt = b.reshape(L // SUBLANES, SUBLANES, LANES)
    sub = lax.broadcasted_iota(jnp.int32, bt.shape, 1)
    out = None
    for j in range(SUBLANES // m):
        idx = j * m + m // 2 - 1
        rj = jnp.broadcast_to(bt[:, idx:idx + 1, :], bt.shape)
        out = rj if out is None else jnp.where(sub >= j * m, rj, out)
    return out.reshape(L, LANES)


def _pair_level(L):
    rowi = lax.broadcasted_iota(jnp.int32, (L, L), 0)
    coli = lax.broadcasted_iota(jnp.int32, (L, L), 1)
    return jnp.where(rowi > coli, 31 - lax.clz(rowi ^ coli), -1)


def _gla_scores(qf, kf, b, L, top, level):
    a_mat = jnp.zeros((L, L), F32)
    m = top
    while m >= 2:
        d = b - _ref_rows(b, m, L)
        f = jnp.exp(jnp.minimum(d, -d))
        am = _nt((qf * f).astype(BF16), (kf * f).astype(BF16))
        a_mat = jnp.where(level == (m // 2).bit_length() - 1, am, a_mat)
        m //= 2
    return a_mat


def _gla_chunk(q, k, v, la, states, gnorm, L):
    rowi = lax.broadcasted_iota(jnp.int32, (L, L), 0)
    coli = lax.broadcasted_iota(jnp.int32, (L, L), 1)
    tril = _one_hot(rowi >= coli)
    level = _pair_level(L)
    scale = DK ** -0.5
    ys, new_states = [], []
    for h in range(HEADS):
        s_mat = states[h]
        qf = q[:, h * DK:(h + 1) * DK].astype(F32) * scale
        kf = k[:, h * DK:(h + 1) * DK].astype(F32)
        vh = v[:, h * DV:(h + 1) * DV]
        b = _sel_left(tril, la[:, h * DK:(h + 1) * DK])
        a_mat = _gla_scores(qf, kf, b, L, L, level)
        diag = jnp.sum(qf * kf, axis=-1, keepdims=True)
        o = (_nn(a_mat.astype(BF16), vh) + diag * vh.astype(F32)
             + _nn((qf * jnp.exp(b)).astype(BF16), s_mat.astype(BF16)))
        ys.append(_head_rmsnorm(o, gnorm[:, h * DV:(h + 1) * DV]))
        b_last = b[L - 1:L, :]
        kd = (kf * jnp.exp(b_last - b)).astype(BF16)
        e_col = jnp.broadcast_to(jnp.exp(b_last), (SUBLANES, LANES)).T[:, 0:1]
        new_states.append(e_col * s_mat + _tn(kd, vh))
    return ys, new_states


def _gla_log_decay(gates, w2_ref, b2_ref):
    return _log_sigmoid(_nn(gates.astype(BF16), w2_ref[...]) + b2_ref[...]) * (1.0 / GLA_TAU)


def _gla_prompt_body(q_ref, k_ref, v_ref, g_ref, w2_ref, b2_ref, gn_ref, y_ref, s_ref, *, L, NC):
    @pl.when(pl.program_id(1) == 0)
    def _():
        s_ref[...] = jnp.zeros_like(s_ref)

    la_all = _gla_log_decay(g_ref[...], w2_ref, b2_ref)
    gnorm = gn_ref[...]
    for c in range(NC):
        sl = slice(c * L, (c + 1) * L)
        states = [s_ref[0, h] for h in range(HEADS)]
        ys, new = _gla_chunk(q_ref[sl, :], k_ref[sl, :], v_ref[sl, :], la_all[sl], states, gnorm, L)
        for h in range(HEADS):
            y_ref[sl, h * DV:(h + 1) * DV] = ys[h].astype(BF16)
            s_ref[0, h] = new[h]


def _gla_prompt(p, gates, w2, b2, gnorm, *, nb, t_len):
    tb = MIX_TB
    nblk = t_len // tb
    rows = lambda width, col: pl.BlockSpec((tb, width), lambda b, t, col=col: (b * nblk + t, col))
    vec = lambda width: pl.BlockSpec((1, width), lambda b, t: (0, 0))
    return pl.pallas_call(
        functools.partial(_gla_prompt_body, L=GLA_CHUNK, NC=tb // GLA_CHUNK),
        grid=(nb, nblk),
        in_specs=[rows(512, C512_GQ), rows(512, C512_GK), rows(1024, C1K_GV), rows(LANES, 0),
                  pl.BlockSpec((LANES, 512), lambda b, t: (0, 0)), vec(512), vec(D_MODEL)],
        out_specs=(rows(D_MODEL, 0), pl.BlockSpec((1, HEADS, DK, DV), lambda b, t: (b, 0, 0, 0))),
        out_shape=(jax.ShapeDtypeStruct((nb * t_len, D_MODEL), BF16), jax.ShapeDtypeStruct((nb, HEADS, DK, DV), F32)),
        compiler_params=_cparams(2, 40), name="gla_prompt",
    )(p, p, p, gates, w2, b2, gnorm)


def _gla_sample_body(*refs, NB):
    s_ref = refs[-1]

    @pl.when(pl.program_id(0) == 0)
    def _():
        _gla_sample_compute(*refs, NB=NB)

    @pl.when(pl.program_id(0) != 0)
    def _():
        s_ref[...] = jnp.zeros_like(s_ref)


def _gla_sample_compute(q_ref, k_ref, v_ref, g_ref, w2_ref, b2_ref, gn_ref, s0_ref, *rest, NB):
    y_ref, s_ref = rest[-2:]
    L = NB * TS
    rowi, coli, same, first, _ = _seq_masks(L, NB)
    tril_seq = _one_hot(jnp.logical_and(same, rowi >= coli))
    full_seq = _one_hot(same)
    level = _pair_level(L)
    la_all = _gla_log_decay(g_ref[...], w2_ref, b2_ref)
    gnorm = gn_ref[...]
    q_all = q_ref[...].astype(F32)
    k_all = k_ref[...].astype(F32)
    v_all = v_ref[...].astype(F32)
    upper = lax.broadcasted_iota(jnp.int32, (WIN, 1), 0) < TS
    scale = DK ** -0.5
    for h in range(HEADS):
        qf = q_all[:, h * DK:(h + 1) * DK] * scale
        kf = k_all[:, h * DK:(h + 1) * DK]
        vf = v_all[:, h * DV:(h + 1) * DV]
        vh = vf.astype(BF16)
        la = la_all[:, h * DK:(h + 1) * DK]
        b = _sel_left(tril_seq, la)
        btot = _sel_left(full_seq, la)
        a_mat = _gla_scores(qf, kf, b, L, TS, level)
        diag = jnp.sum(qf * kf, axis=-1, keepdims=True)
        qe = qf * jnp.exp(b)
        inter = []
        for wd in range(NB // 2):
            qw = qe[WIN * wd:WIN * (wd + 1)].astype(BF16)
            o0 = _nn(qw, s0_ref[0, 2 * wd, h].astype(BF16))
            o1 = _nn(qw, s0_ref[0, 2 * wd + 1, h].astype(BF16))
            inter.append(jnp.where(upper, o0, o1))
        o = _nn(a_mat.astype(BF16), vh) + diag * vf + jnp.concatenate(inter, axis=0)
        y_ref[:, h * DV:(h + 1) * DV] = _head_rmsnorm(o, gnorm[:, h * DV:(h + 1) * DV]).astype(BF16)
        kd = kf * jnp.exp(btot - b)
        e_cols = jnp.exp(_sel_left(first, btot)).T
        for bq in range(NB):
            wd, par = bq // 2, bq % 2
            valid = upper if par == 0 else jnp.logical_not(upper)
            kwin = jnp.where(valid, kd[WIN * wd:WIN * (wd + 1)], 0.0).astype(BF16)
            vwin = vf[WIN * wd:WIN * (wd + 1)].astype(BF16)
            s_ref[0, bq, h] = e_cols[:, bq:bq + 1] * s0_ref[0, bq, h] + _tn(kwin, vwin)


def _gla_sample(p, gates, w2, b2, gnorm, s0, s_prev, *, layer, n_seq):
    nbk = SAMPLE_NB
    rb = nbk * TS
    has_prev = s_prev is not None
    grid, blk, lay = _sample_grid(n_seq, s0.shape[0], layer, has_prev)
    rows = lambda width, col: pl.BlockSpec((rb, width), lambda s, i, col=col: (blk(s, i), col))
    vec = lambda width: pl.BlockSpec((1, width), lambda s, i: (0, 0))
    in_specs = [rows(512, C512_GQ), rows(512, C512_GK), rows(1024, C1K_GV), rows(LANES, 0),
                pl.BlockSpec((LANES, 512), lambda s, i: (0, 0)), vec(512), vec(D_MODEL),
                pl.BlockSpec((1, nbk, HEADS, DK, DV), lambda s, i: (layer, blk(s, i), 0, 0, 0))]
    args = [p, p, p, gates, w2, b2, gnorm, s0]
    aliases = {}
    if has_prev:
        in_specs.append(pl.BlockSpec(memory_space=pl.ANY))
        args.append(s_prev)
        aliases[len(args) - 1] = 1
    return pl.pallas_call(
        functools.partial(_gla_sample_body, NB=nbk),
        grid=grid,
        in_specs=in_specs,
        out_specs=(rows(D_MODEL, 0), pl.BlockSpec((1, nbk, HEADS, DK, DV), lambda s, i: (lay(s), i, 0, 0, 0))),
        out_shape=(jax.ShapeDtypeStruct((n_seq * TS, D_MODEL), BF16), jax.ShapeDtypeStruct(s0.shape, F32)),
        input_output_aliases=aliases,
        compiler_params=_cparams(2, 56), name="gla_sample",
    )(*args)


def _lru_gates(xc, wa_ref, ba_ref, wi_ref, bi_ref, lam_ref):
    ra, ri = [], []
    for nb in range(LRU_BLOCKS):
        xs = xc[:, nb * LRU_BW:(nb + 1) * LRU_BW].astype(BF16)
        ra.append(_nn(xs, wa_ref[nb]))
        ri.append(_nn(xs, wi_ref[nb]))
    r = _sigmoid(jnp.concatenate(ra, axis=-1) + ba_ref[...])
    ig = _sigmoid(jnp.concatenate(ri, axis=-1) + bi_ref[...])
    a = jnp.exp(((-LRU_C) * _softplus(-lam_ref[...])) * r)
    om = 1.0 - a * a
    mult = om * lax.rsqrt(jnp.maximum(om, 1e-30))
    return a, ig * xc, mult


def _conv_shift_matrices(tb):
    nsh = CONV_W - 1
    j = jnp.arange(nsh)[:, None, None] + 1
    t = jnp.arange(tb)[None, :, None]
    s = jnp.arange(tb)[None, None, :]
    shift = (t - s == j).astype(BF16).reshape(nsh * tb, tb)
    t8 = jnp.arange(SUBLANES)[None, :, None]
    s8 = jnp.arange(SUBLANES)[None, None, :]
    tail = (s8 - t8 == SUBLANES - j).astype(BF16).reshape(nsh * SUBLANES, SUBLANES)
    return shift, tail


def _lru_prompt_body(x_ref, shift_ref, tail_ref, wc_ref, bc_ref, wa_ref, ba_ref, wi_ref, bi_ref, lam_ref,
                     h_ref, hfin_ref, cfin_ref, xprev, hcar, *, TB):
    t = pl.program_id(1)

    @pl.when(t == 0)
    def _():
        xprev[...] = jnp.zeros_like(xprev)
        hcar[...] = jnp.zeros_like(hcar)

    xb = x_ref[...]
    x = xb.astype(F32)
    nsh = CONV_W - 1
    shifted = _nn(shift_ref[...], xb)
    from_prev = _nn(tail_ref[...], xprev[...].astype(BF16))
    wc = wc_ref[...]
    xc = bc_ref[...] + wc[CONV_W - 1:CONV_W, :] * x
    for j in range(nsh):
        sj = shifted[j * TB:(j + 1) * TB]
        sj = jnp.concatenate([sj[0:SUBLANES] + from_prev[j * SUBLANES:(j + 1) * SUBLANES], sj[SUBLANES:]], axis=0)
        xc = xc + wc[CONV_W - 2 - j:CONV_W - 1 - j, :] * sj
    xprev[...] = x[TB - SUBLANES:TB, :]

    a, gx, mult = _lru_gates(xc, wa_ref, ba_ref, wi_ref, bi_ref, lam_ref)
    row8 = lax.broadcasted_iota(jnp.int32, (SUBLANES, 1), 0)
    m_first = jnp.where(row8 + t == 0, 1.0, mult[0:SUBLANES])
    u = jnp.concatenate([m_first, mult[SUBLANES:]], axis=0) * gx
    nt8 = TB // SUBLANES
    a3 = a.reshape(nt8, SUBLANES, D_MODEL)
    u3 = u.reshape(nt8, SUBLANES, D_MODEL)
    sub = lax.broadcasted_iota(jnp.int32, (1, SUBLANES, D_MODEL), 1)
    d = 1
    while d < SUBLANES:
        keep = sub >= d
        a_sh = jnp.where(keep, pltpu.roll(a3, d, 1), 1.0)
        u_sh = jnp.where(keep, pltpu.roll(u3, d, 1), 0.0)
        u3 = a3 * u_sh + u3
        a3 = a3 * a_sh
        d *= 2
    h_in = hcar[...]
    hs = []
    for j in range(nt8):
        hj = a3[j] * h_in + u3[j]
        hs.append(hj)
        h_in = hj[SUBLANES - 1:SUBLANES]
    h_ref[...] = jnp.concatenate(hs, axis=0).astype(BF16)
    hcar[...] = h_in
    hfin_ref[0] = h_in
    cfin_ref[0] = x[TB - (CONV_W - 1):TB, :]


def _lru_prompt(p, wc, bc, wa, ba, wi, bi, lam, *, nb, t_len):
    tb = LRU_TB
    nblk = t_len // tb
    vec = pl.BlockSpec((1, D_MODEL), lambda b, t: (0, 0))
    wblk = pl.BlockSpec((LRU_BLOCKS, LRU_BW, LRU_BW), lambda b, t: (0, 0, 0))
    shift, tail = _conv_shift_matrices(tb)
    whole = lambda arr: pl.BlockSpec(arr.shape, lambda b, t: (0, 0))
    return pl.pallas_call(
        functools.partial(_lru_prompt_body, TB=tb),
        grid=(nb, nblk),
        in_specs=[pl.BlockSpec((tb, D_MODEL), lambda b, t: (b * nblk + t, C1K_LX)), whole(shift), whole(tail),
                  pl.BlockSpec((CONV_W, D_MODEL), lambda b, t: (0, 0)), vec, wblk, vec, wblk, vec, vec],
        out_specs=(pl.BlockSpec((tb, D_MODEL), lambda b, t: (b * nblk + t, 0)),
                   pl.BlockSpec((1, 1, D_MODEL), lambda b, t: (b, 0, 0)),
                   pl.BlockSpec((1, CONV_W - 1, D_MODEL), lambda b, t: (b, 0, 0))),
        out_shape=(jax.ShapeDtypeStruct((nb * t_len, D_MODEL), BF16),
                   jax.ShapeDtypeStruct((nb, 1, D_MODEL), F32),
                   jax.ShapeDtypeStruct((nb, CONV_W - 1, D_MODEL), F32)),
        scratch_shapes=[pltpu.VMEM((SUBLANES, D_MODEL), F32), pltpu.VMEM((1, D_MODEL), F32)],
        compiler_params=_cparams(2, 40), name="lru_prompt",
    )(p, shift, tail, wc, bc, wa, ba, wi, bi, lam)


def _lru_sample_body(x_ref, buf_ref, h0_ref, wc_ref, bc_ref, wa_ref, ba_ref, wi_ref, bi_ref, lam_ref,
                     h_ref, hfin_ref, cfin_ref, *, T):
    wc = wc_ref[...]
    xs = [buf_ref[j] for j in range(CONV_W - 1)] + [x_ref[t].astype(F32) for t in range(T)]
    h = h0_ref[...]
    for t in range(T):
        xc = bc_ref[...]
        for j in range(CONV_W):
            xc = xc + wc[j:j + 1, :] * xs[t + j]
        a, gx, mult = _lru_gates(xc, wa_ref, ba_ref, wi_ref, bi_ref, lam_ref)
        h = a * h + mult * gx
        h_ref[t] = h.astype(BF16)
    hfin_ref[...] = h
    for j in range(CONV_W - 1):
        cfin_ref[j] = xs[T + j]


def _lru_sample(x_tm, buf_tm, h0, wc, bc, wa, ba, wi, bi, lam):
    t_len, n_seq, _ = x_tm.shape
    return pl.pallas_call(
        functools.partial(_lru_sample_body, T=t_len),
        out_shape=(jax.ShapeDtypeStruct((t_len, n_seq, D_MODEL), BF16),
                   jax.ShapeDtypeStruct((n_seq, D_MODEL), F32),
                   jax.ShapeDtypeStruct((CONV_W - 1, n_seq, D_MODEL), F32)),
        compiler_params=pltpu.CompilerParams(vmem_limit_bytes=40 * 1024 * 1024), name="lru_sample",
    )(x_tm, buf_tm, h0, wc, bc, wa, ba, wi, bi, lam)


def _gate_weights(w):
    col = lambda r: w[:, r[0]:r[1]]
    pad = jnp.zeros((D_MODEL, LANES - 2 * HEADS - GLA_RANK), F32)
    return jnp.concatenate([col(W_IN_IF), col(W_IN_LR), pad], axis=1).astype(BF16)


def kernel(x_prompt, x_sample, state_mlstm_C, state_mlstm_n, state_mlstm_m, state_gla_S, state_lru_h, state_lru_conv,
           norm_ffn1, w_ffn1_in, w_ffn1_out, norm_mix, w_in, b_ml_if, g_ml_norm, w_gla_lr2, b_gla_gate, g_gla_norm,
           w_conv, b_conv, w_lru_a, b_lru_a, w_lru_i, b_lru_i, lru_lambda, w_out, norm_ffn2, w_ffn2_in, w_ffn2_out,
           norm_final):
    nb, t_len, d = x_prompt.shape
    n_seq, ts, _ = x_sample.shape
    depth = w_in.shape[0]
    assert d == D_MODEL and ts == TS and t_len % MIX_TB == 0 and n_seq % SAMPLE_NB == 0

    xp = x_prompt.reshape(nb * t_len, d)
    xs = x_sample.reshape(n_seq * ts, d)

    new_p = {k: [] for k in ("C", "n", "m", "S", "h", "conv")}
    new_s = {k: [] for k in ("n", "m", "h", "conv")}
    s_c_all, s_s_all = None, None
    yp = ys = None
    for l in range(depth):
        w_small = _gate_weights(w_in[l])
        bias_if = jnp.concatenate([b_ml_if[l], jnp.zeros((LANES - 2 * HEADS,), F32)]).reshape(1, LANES)
        w2 = jnp.zeros((LANES, HEADS * DK), F32).at[2 * HEADS:2 * HEADS + GLA_RANK].set(w_gla_lr2[l]).astype(BF16)
        b2 = b_gla_gate[l].reshape(1, HEADS * DK)
        g_ml = g_ml_norm[l].reshape(1, d)
        g_gla = g_gla_norm[l].reshape(1, d)
        wc, bc = w_conv[l], b_conv[l].reshape(1, d)
        wa, wi = w_lru_a[l].astype(BF16), w_lru_i[l].astype(BF16)
        ba, bi, lam = b_lru_a[l].reshape(1, d), b_lru_i[l].reshape(1, d), lru_lambda[l].reshape(1, d)
        f1_in, f1_out = w_ffn1_in[l].astype(BF16), w_ffn1_out[l].astype(BF16)
        f2_in, f2_out = w_ffn2_in[l].astype(BF16), w_ffn2_out[l].astype(BF16)
        w_o = w_out[l].astype(BF16)
        last = l == depth - 1
        g_next = norm_final if last else norm_ffn1[l + 1]

        x1, xn, gates = _ffn(xp, norm_ffn1[l], f1_in, f1_out, norm_mix[l], final=False, w_small=w_small)
        p = _proj(xn, w_in, l)
        g_ml_rep = jnp.broadcast_to(g_ml_norm[l][:, None], (d, LANES))
        y_ml, p_c, p_n, p_m = _mlstm_prompt(p, gates, bias_if, g_ml_rep, nb=nb, t_len=t_len)
        y_gla, p_s = _gla_prompt(p, gates, w2, b2, g_gla, nb=nb, t_len=t_len)
        h_lru, p_h, p_conv = _lru_prompt(p, wc, bc, wa, ba, wi, bi, lam, nb=nb, t_len=t_len)
        x2 = _merge(x1, p, y_ml, y_gla, h_lru, w_o)
        if last:
            yp = _ffn(x2, norm_ffn2[l], f2_in, f2_out, g_next, final=True)
        else:
            xp, _ = _ffn(x2, norm_ffn2[l], f2_in, f2_out, g_next, final=False)

        x1, xn, gates = _ffn(xs, norm_ffn1[l], f1_in, f1_out, norm_mix[l], final=False, w_small=w_small)
        p = _proj(xn, w_in, l)
        n0_hm = state_mlstm_n[l].transpose(1, 0, 2)
        n_rep = jnp.repeat(state_mlstm_n[l].reshape(n_seq, HEADS * DK), ts, axis=0)
        m_rep = jnp.repeat(jnp.pad(state_mlstm_m[l], ((0, 0), (0, LANES - HEADS))), ts, axis=0)
        y_ml, s_c_all, s_n, s_m = _mlstm_sample(p, gates, bias_if, g_ml, state_mlstm_C, n0_hm, n_rep, m_rep,
                                                s_c_all, layer=l, n_seq=n_seq)
        y_gla, s_s_all = _gla_sample(p, gates, w2, b2, g_gla, state_gla_S, s_s_all, layer=l, n_seq=n_seq)
        lx_s = p[:, C1K_LX * 1024:(C1K_LX + 1) * 1024].reshape(n_seq, ts, d).transpose(1, 0, 2)
        hs_tm, s_h, s_conv_tm = _lru_sample(lx_s, state_lru_conv[l].transpose(1, 0, 2), state_lru_h[l],
                                            wc, bc, wa, ba, wi, bi, lam)
        h_lru = hs_tm.transpose(1, 0, 2).reshape(n_seq * ts, d)
        x2 = _merge(x1, p, y_ml, y_gla, h_lru, w_o)
        if last:
            ys = _ffn(x2, norm_ffn2[l], f2_in, f2_out, g_next, final=True)
        else:
            xs, _ = _ffn(x2, norm_ffn2[l], f2_in, f2_out, g_next, final=False)

        new_p["C"].append(p_c)
        new_p["n"].append(p_n)
        new_p["m"].append(p_m[:, :, 0])
        new_p["S"].append(p_s)
        new_p["h"].append(p_h[:, 0, :])
        new_p["conv"].append(p_conv)
        new_s["n"].append(s_n.transpose(1, 0, 2))
        new_s["m"].append(s_m[:, :, 0].T)
        new_s["h"].append(s_h)
        new_s["conv"].append(s_conv_tm.transpose(1, 0, 2))

    st = lambda arrs: jnp.stack(arrs)
    return (yp.reshape(nb, t_len, d), ys.reshape(n_seq, ts, d),
            st(new_p["C"]), st(new_p["n"]), st(new_p["m"]), st(new_p["S"]), st(new_p["h"]), st(new_p["conv"]),
            s_c_all, st(new_s["n"]), st(new_s["m"]), s_s_all, st(new_s["h"]), st(new_s["conv"]))
```

```python
import functools

import jax
import jax.numpy as jnp
from jax import lax
from jax.experimental import pallas as pl
from jax.experimental.pallas import tpu as pltpu

F32 = jnp.float32
BF16 = jnp.bfloat16

D_MODEL = 1024
D_FF = 2816
HEADS = 4
DK = 128
DV = 256
GLA_RANK = 16
GLA_TAU = 16.0
LRU_BLOCKS = 8
LRU_BW = 128
LRU_C = 8.0
CONV_W = 4
EPS = 1e-6
NEG = -1e30
LOG2E = 1.4426950408889634

LANES = 128
SUBLANES = 8
MXU_N = 256

P_COLS = 11264
C512_MQ, C512_MK, C512_GQ, C512_GK = 0, 1, 6, 7
C1K_MV, C1K_MO, C1K_GV, C1K_GG, C1K_LX, C1K_LY, C1K_MG0 = 1, 2, 4, 5, 6, 7, 8
W_IN_A = (0, 3072)
W_IN_IF = (3072, 3080)
W_IN_B = (3080, 6152)
W_IN_LR = (6152, 6168)
W_IN_C = (6168, 11288)
assert W_IN_IF[0] % 128 == 0 and W_IN_LR[0] % 128 == W_IN_IF[1] - W_IN_IF[0]
PROJ_TN = 1024
PROJ_WIN = PROJ_TN // 128 + 1
PROJ_REGIONS = ((0, 3, 0), (3, 6, W_IN_B[0] - W_IN_A[1]), (6, 11, W_IN_C[0] - W_IN_B[1] + W_IN_B[0] - W_IN_A[1]))

ML_CHUNK = 256
GLA_CHUNK = 128
MIX_TB = 512
LRU_TB = 256
TS = 4
TS_SHIFT = 2
SAMPLE_NB = 16
WIN = 8
FF_CHUNK = MXU_N


def _cparams(n_axes, vmem_mib):
    return pltpu.CompilerParams(dimension_semantics=("arbitrary",) * n_axes,
                                vmem_limit_bytes=vmem_mib * 1024 * 1024)


def _pick_tile(n, target, mult):
    best = None
    for t in range(mult, min(n, target) + 1, mult):
        if n % t == 0:
            best = t
    assert best is not None, (n, target, mult)
    return best


def _sigmoid(x):
    return 0.5 * jnp.tanh(0.5 * x) + 0.5


def _log_sigmoid(x):
    return jnp.minimum(x, 0.0) - jnp.log(1.0 + jnp.exp(-jnp.abs(x)))


def _softplus(x):
    return jnp.maximum(x, 0.0) + jnp.log(1.0 + jnp.exp(-jnp.abs(x)))


def _rms(x, g):
    return x * lax.rsqrt(jnp.mean(x * x, axis=-1, keepdims=True) + EPS) * g


def _nn(a, b):
    return jnp.dot(a, b, preferred_element_type=F32)


def _nt(a, b):
    return lax.dot_general(a, b, (((1,), (1,)), ((), ())), preferred_element_type=F32)


def _tn(a, b):
    return lax.dot_general(a, b, (((0,), (0,)), ((), ())), preferred_element_type=F32)


def _split3(x):
    hi = x.astype(BF16)
    r1 = x - hi.astype(F32)
    mid = r1.astype(BF16)
    lo = (r1 - mid.astype(F32)).astype(BF16)
    return hi, mid, lo


def _sel_left(sel, x):
    hi, mid, lo = _split3(x)
    return _nn(sel, hi) + _nn(sel, mid) + _nn(sel, lo)


def _sel_right(x, sel):
    hi, mid, lo = _split3(x)
    return _nn(hi, sel) + _nn(mid, sel) + _nn(lo, sel)


def _sel_left2(sel, x):
    hi = x.astype(BF16)
    lo = (x - hi.astype(F32)).astype(BF16)
    return _nn(sel, hi) + _nn(sel, lo)


def _one_hot(mask):
    return jnp.where(mask, 1.0, 0.0).astype(BF16)


def _head_layernorm(h, g):
    mu = jnp.mean(h, axis=-1, keepdims=True)
    hc = h - mu
    return hc * lax.rsqrt(jnp.mean(hc * hc, axis=-1, keepdims=True) + EPS) * g


def _head_rmsnorm(h, g):
    return h * lax.rsqrt(jnp.mean(h * h, axis=-1, keepdims=True) + EPS) * g


def _ffn_body(x_ref, g_ref, win_ref, wout_ref, gn_ref, *rest, n_chunks, final, with_gates):
    if with_gates:
        wif_ref, wlr_ref, rest = rest[0], rest[1], rest[2:]
    if final:
        y_ref, xn_s, acc_s = rest
    elif with_gates:
        y_ref, xn_ref, gates_ref, xn_s, acc_s = rest
    else:
        y_ref, xn_ref, xn_s, acc_s = rest
    xn_s[...] = _rms(x_ref[...], g_ref[...]).astype(BF16)
    for c in range(n_chunks):
        xn = xn_s[...]
        g = _nn(xn, win_ref[:, c * FF_CHUNK:(c + 1) * FF_CHUNK])
        u = _nn(xn, win_ref[:, D_FF + c * FF_CHUNK:D_FF + (c + 1) * FF_CHUNK])
        t = jnp.tanh(0.5 * g)
        a = (g * (t + 1.0)) * u
        part = _nn(a.astype(BF16), wout_ref[c * FF_CHUNK:(c + 1) * FF_CHUNK, :])
        if c == 0:
            acc_s[...] = part
        else:
            acc_s[...] += part
    xo = x_ref[...] + 0.25 * acc_s[...]
    if final:
        y_ref[...] = _rms(xo, gn_ref[...])
    else:
        y_ref[...] = xo
        xn_next = _rms(xo, gn_ref[...]).astype(BF16)
        xn_ref[...] = xn_next
        if with_gates:
            w_pair = jnp.concatenate([wif_ref[...], wlr_ref[...]], axis=1).astype(BF16)
            g_pair = _nn(xn_next, w_pair)
            lane = lax.broadcasted_iota(jnp.int32, (g_pair.shape[0], LANES), 1)
            gates_ref[...] = jnp.where(lane < 2 * HEADS, g_pair[:, 0:LANES], g_pair[:, LANES:2 * LANES])


def _ffn(x, g, w_in_c, w_out_c, g_next, *, layer, final, w_gate=None):
    rows = x.shape[0]
    tm = _pick_tile(rows, 1024, 16)
    n_chunks = D_FF // FF_CHUNK
    with_gates = w_gate is not None
    row_spec = pl.BlockSpec((tm, D_MODEL), lambda i: (i, 0))
    vec = pl.BlockSpec((1, D_MODEL), lambda i: (0, 0))
    of_layer = lambda arr: pl.BlockSpec((None,) + arr.shape[1:], lambda i: (layer, 0, 0), pipeline_mode=pl.Buffered(1))
    in_specs = [row_spec, vec, of_layer(w_in_c), of_layer(w_out_c), vec]
    args = [x, g.reshape(1, D_MODEL), w_in_c, w_out_c, g_next.reshape(1, D_MODEL)]
    if final:
        out_shape = jax.ShapeDtypeStruct((rows, D_MODEL), F32)
        out_specs = row_spec
    else:
        out_shape = [jax.ShapeDtypeStruct((rows, D_MODEL), F32), jax.ShapeDtypeStruct((rows, D_MODEL), BF16)]
        out_specs = [row_spec, row_spec]
        if with_gates:
            for col in (W_IN_IF[0], W_IN_LR[0]):
                in_specs.append(pl.BlockSpec((None, D_MODEL, LANES), lambda i, col=col: (layer, 0, col // LANES),
                                             pipeline_mode=pl.Buffered(1)))
                args.append(w_gate)
            out_shape.append(jax.ShapeDtypeStruct((rows, LANES), F32))
            out_specs.append(pl.BlockSpec((tm, LANES), lambda i: (i, 0)))
    return pl.pallas_call(
        functools.partial(_ffn_body, n_chunks=n_chunks, final=final, with_gates=with_gates),
        grid=(rows // tm,),
        in_specs=in_specs, out_specs=out_specs, out_shape=out_shape,
        scratch_shapes=[pltpu.VMEM((tm, D_MODEL), BF16), pltpu.VMEM((tm, D_MODEL), F32)],
        compiler_params=_cparams(1, 56), name="ffn_final" if final else "ffn",
    )(*args)


def _proj_body(xn_ref, *rest):
    w_refs, p_ref, w_s = rest[:PROJ_WIN], rest[PROJ_WIN], rest[PROJ_WIN + 1]
    j = pl.program_id(0)

    def build(shift):
        cat = jnp.concatenate([r[...] for r in w_refs], axis=1)
        w_s[...] = cat[:, shift:shift + PROJ_TN].astype(BF16)

    first_row_tile = pl.program_id(1) == 0
    for lo, hi, shift in PROJ_REGIONS:
        pl.when(jnp.logical_and(first_row_tile, jnp.logical_and(j >= lo, j < hi)))(functools.partial(build, shift))
    p_ref[...] = _nn(xn_ref[...], w_s[...]).astype(BF16)


def _proj(xn, w_in_all, layer):
    rows = xn.shape[0]
    tm = _pick_tile(rows, 2048, 16)
    wblk = lambda k: pl.BlockSpec((None, D_MODEL, LANES), lambda j, i, k=k: (layer, 0, j * (PROJ_TN // LANES) + k))
    return pl.pallas_call(
        _proj_body, grid=(P_COLS // PROJ_TN, rows // tm),
        in_specs=[pl.BlockSpec((tm, D_MODEL), lambda j, i: (i, 0))] + [wblk(k) for k in range(PROJ_WIN)],
        out_specs=pl.BlockSpec((tm, PROJ_TN), lambda j, i: (i, j)),
        out_shape=jax.ShapeDtypeStruct((rows, P_COLS), BF16),
        scratch_shapes=[pltpu.VMEM((D_MODEL, PROJ_TN), BF16)],
        compiler_params=_cparams(2, 48), name="proj",
    )(xn, *([w_in_all] * PROJ_WIN))


def _merge_body(x_ref, mo_ref, gg_ref, ly_ref, m0_ref, m1_ref, m2_ref, yml_ref, ygla_ref, hlru_ref, w_ref, o_ref):
    f = lambda r: r[...].astype(F32)
    th = lambda r: jnp.tanh(0.5 * f(r)) + 1.0
    gg = f(gg_ref)
    ly = f(ly_ref)
    y_ml = f(yml_ref) * th(mo_ref)
    y_gla = f(ygla_ref) * (gg * th(gg_ref))
    y_lru = f(hlru_ref) * (ly * (1.0 + jnp.tanh(0.7978845608028654 * (ly + 0.044715 * (ly * ly * ly)))))
    merged4 = th(m0_ref) * y_ml + th(m1_ref) * y_gla + th(m2_ref) * y_lru
    o_ref[...] = x_ref[...] + _nn((0.25 * merged4).astype(BF16), w_ref[...])


def _merge(x, p, y_ml, y_gla, h_lru, w_out, layer):
    rows = x.shape[0]
    tm = _pick_tile(rows, 512, 16)
    row = lambda c: pl.BlockSpec((tm, D_MODEL), lambda i, c=c: (i, c))
    return pl.pallas_call(
        _merge_body, grid=(rows // tm,),
        in_specs=[row(0), row(C1K_MO), row(C1K_GG), row(C1K_LY), row(C1K_MG0), row(C1K_MG0 + 1), row(C1K_MG0 + 2),
                  row(0), row(0), row(0), pl.BlockSpec((None, D_MODEL, D_MODEL), lambda i: (layer, 0, 0))],
        out_specs=row(0),
        out_shape=jax.ShapeDtypeStruct((rows, D_MODEL), F32),
        compiler_params=_cparams(1, 56), name="merge",
    )(x, p, p, p, p, p, p, y_ml, y_gla, h_lru, w_out)


def _mlstm_chunk(q, k, v, gi, lf, states, g_rep, L):
    rowi = lax.broadcasted_iota(jnp.int32, (L, L), 0)
    coli = lax.broadcasted_iota(jnp.int32, (L, L), 1)
    upper = rowi <= coli
    b_cols = _sel_left(_one_hot(rowi >= coli), lf)
    lf_t = lf.T[0:SUBLANES]
    b_rows = _sel_right(lf_t, _one_hot(upper))
    scale = DK ** -0.5
    ys, new_states = [], []
    for h in range(HEADS):
        c_mat, n_vec, m_prev = states[h]
        qh = q[:, h * DK:(h + 1) * DK]
        kh = k[:, h * DK:(h + 1) * DK]
        vh = v[:, h * DV:(h + 1) * DV]
        b_col = b_cols[:, HEADS + h:HEADS + h + 1]
        b_row = b_rows[HEADS + h:HEADS + h + 1, :]
        c_col = gi[:, h:h + 1] - b_col
        dm = jnp.where(upper, c_col, NEG)
        big_m = jnp.maximum(m_prev, jnp.max(dm, axis=0, keepdims=True))
        w = (_nt(kh, qh) * scale) * jnp.exp(dm - big_m)
        s_inter = jnp.exp(m_prev - big_m)
        num = _tn(vh, w.astype(BF16)) + s_inter * _nt(c_mat.astype(BF16), qh)
        qn = _nt(jnp.broadcast_to(n_vec, (SUBLANES, DK)).astype(BF16), qh)[0:1, :]
        den = jnp.sum(w, axis=0, keepdims=True) + s_inter * qn
        hh = num / jnp.maximum(jnp.abs(den), jnp.exp(-(b_row + big_m)))
        mu = jnp.mean(hh, axis=0, keepdims=True)
        hc = hh - mu
        g_h = jnp.concatenate([g_rep[h * DV:(h + 1) * DV]] * (L // LANES), axis=1)
        y_t = hc * lax.rsqrt(jnp.mean(hc * hc, axis=0, keepdims=True) + EPS) * g_h
        ys.append(y_t.astype(BF16).T)
        m_last = big_m[:, L - 1:L]
        wk = jnp.exp(c_col - m_last)
        decay = jnp.exp(m_prev - m_last)
        kw = kh.astype(F32) * (wk * scale)
        c_new = decay * c_mat + _tn(vh, kw.astype(BF16))
        n_new = decay * n_vec + jnp.sum(kw, axis=0, keepdims=True)
        m_new = b_col[L - 1:L, :] + m_last
        new_states.append((c_new, n_new, m_new))
    return ys, new_states


def _mlstm_prompt_body(q_ref, k_ref, v_ref, g_ref, bias_ref, gn_ref, y_ref, c_ref, n_ref, m_ref, *, L, NC):
    @pl.when(pl.program_id(1) == 0)
    def _():
        c_ref[...] = jnp.zeros_like(c_ref)
        n_ref[...] = jnp.zeros_like(n_ref)
        m_ref[...] = jnp.zeros_like(m_ref)

    gates = g_ref[...] + bias_ref[...]
    lf_all = _log_sigmoid(gates)
    gnorm = gn_ref[...]
    for c in range(NC):
        sl = slice(c * L, (c + 1) * L)
        states = [(c_ref[0, h], n_ref[0, h:h + 1, :], m_ref[0, h:h + 1, 0:1]) for h in range(HEADS)]
        ys, new = _mlstm_chunk(q_ref[sl, :], k_ref[sl, :], v_ref[sl, :], gates[sl], lf_all[sl], states, gnorm, L)
        for h in range(HEADS):
            y_ref[sl, h * DV:(h + 1) * DV] = ys[h]
            c_ref[0, h] = new[h][0]
            n_ref[0, h:h + 1, :] = new[h][1]
            m_ref[0, h:h + 1, :] = jnp.broadcast_to(new[h][2], (1, LANES))


def _mlstm_prompt(p, gates, bias, g_rep, *, nb, t_len):
    tb = MIX_TB
    nblk = t_len // tb
    rows = lambda width, col: pl.BlockSpec((tb, width), lambda b, t, col=col: (b * nblk + t, col))
    vec = lambda width: pl.BlockSpec((1, width), lambda b, t: (0, 0))
    return pl.pallas_call(
        functools.partial(_mlstm_prompt_body, L=ML_CHUNK, NC=tb // ML_CHUNK),
        grid=(nb, nblk),
        in_specs=[rows(512, C512_MQ), rows(512, C512_MK), rows(1024, C1K_MV), rows(LANES, 0), vec(LANES),
                  pl.BlockSpec((D_MODEL, LANES), lambda b, t: (0, 0))],
        out_specs=(rows(D_MODEL, 0),
                   pl.BlockSpec((1, HEADS, DV, DK), lambda b, t: (b, 0, 0, 0)),
                   pl.BlockSpec((1, HEADS, DK), lambda b, t: (b, 0, 0)),
                   pl.BlockSpec((1, HEADS, LANES), lambda b, t: (b, 0, 0))),
        out_shape=(jax.ShapeDtypeStruct((nb * t_len, D_MODEL), BF16),
                   jax.ShapeDtypeStruct((nb, HEADS, DV, DK), F32),
                   jax.ShapeDtypeStruct((nb, HEADS, DK), F32),
                   jax.ShapeDtypeStruct((nb, HEADS, LANES), F32)),
        compiler_params=_cparams(2, 40), name="mlstm_prompt",
    )(p, p, p, gates, bias, g_rep)


def _seq_masks(L, nseq):
    rowi = lax.broadcasted_iota(jnp.int32, (L, L), 0)
    coli = lax.broadcasted_iota(jnp.int32, (L, L), 1)
    same = (rowi >> TS_SHIFT) == (coli >> TS_SHIFT)
    sb = lax.broadcasted_iota(jnp.int32, (nseq, L), 0)
    sr = lax.broadcasted_iota(jnp.int32, (nseq, L), 1)
    first = _one_hot(sr == sb * TS)
    member = _one_hot((sr >> TS_SHIFT) == sb)
    return rowi, coli, same, first, member


def _mlstm_sample_body(*refs, NB):
    c_ref = refs[-3]

    @pl.when(pl.program_id(0) == 0)
    def _():
        _mlstm_sample_compute(*refs, NB=NB)

    @pl.when(pl.program_id(0) != 0)
    def _():
        c_ref[...] = jnp.zeros_like(c_ref)


def _mlstm_sample_compute(q_ref, k_ref, v_ref, g_ref, bias_ref, gn_ref, c0_ref, n0_ref, nrep_ref, mrep_ref, *rest, NB):
    y_ref, c_ref, n_ref, m_ref = rest[-4:]
    L = NB * TS
    rowi, coli, same, first, member = _seq_masks(L, NB)
    causal = jnp.logical_and(same, rowi >= coli)
    gates = g_ref[...] + bias_ref[...]
    lf = _log_sigmoid(gates)
    b_cols = _sel_left(_one_hot(causal), lf)
    btot_cols = _sel_left(_one_hot(same), lf)
    gi_t = gates.T[0:SUBLANES]
    lf_t = lf.T[0:SUBLANES]
    b_rows = _sel_right(lf_t, _one_hot(jnp.logical_and(same, rowi <= coli)))
    gnorm = gn_ref[...]
    q_all = q_ref[...].astype(F32)
    k_all = k_ref[...].astype(F32)
    v_all = v_ref[...].astype(F32)
    m_rep = mrep_ref[...]
    lane = lax.broadcasted_iota(jnp.int32, (L, LANES), 1)
    upper = lax.broadcasted_iota(jnp.int32, (WIN, 1), 0) < TS
    scale = DK ** -0.5
    for h in range(HEADS):
        qf = q_all[:, h * DK:(h + 1) * DK]
        kf = k_all[:, h * DK:(h + 1) * DK]
        vf = v_all[:, h * DV:(h + 1) * DV]
        qh, kh, vh = qf.astype(BF16), kf.astype(BF16), vf.astype(BF16)
        i_col = gates[:, h:h + 1]
        b_col = b_cols[:, HEADS + h:HEADS + h + 1]
        c_row = gi_t[h:h + 1, :] - b_rows[HEADS + h:HEADS + h + 1, :]
        m_prev = m_rep[:, h:h + 1]
        dm = jnp.where(causal, c_row, NEG)
        big_m = jnp.maximum(m_prev, jnp.max(dm, axis=-1, keepdims=True))
        m_last = jnp.maximum(m_prev, jnp.max(jnp.where(same, c_row, NEG), axis=-1, keepdims=True))
        pmat = jnp.exp(dm - big_m)
        w = (_nt(qh, kh) * scale) * pmat
        s_inter = jnp.exp(m_prev - big_m)
        qc = []
        for wd in range(NB // 2):
            qw = qf[WIN * wd:WIN * (wd + 1)].astype(BF16)
            o0 = _nt(qw, c0_ref[0, 2 * wd, h].astype(BF16))
            o1 = _nt(qw, c0_ref[0, 2 * wd + 1, h].astype(BF16))
            qc.append(jnp.where(upper, o0, o1))
        num = _nn(w.astype(BF16), vh) + s_inter * jnp.concatenate(qc, axis=0)
        qn = jnp.sum(qf * nrep_ref[:, h * DK:(h + 1) * DK], axis=-1, keepdims=True)
        den = jnp.sum(w, axis=-1, keepdims=True) + s_inter * qn
        hh = num / jnp.maximum(jnp.abs(den), jnp.exp(-(b_col + big_m)))
        y_ref[:, h * DV:(h + 1) * DV] = _head_layernorm(hh, gnorm[:, h * DV:(h + 1) * DV]).astype(BF16)
        wk = jnp.exp((i_col - b_col) - m_last)
        decay_col = jnp.exp(m_prev - m_last)
        kw = kf * (wk * scale)
        m_new_col = btot_cols[:, HEADS + h:HEADS + h + 1] + m_last
        z = jnp.where(lane == 0, decay_col, jnp.where(lane == 1, m_new_col, 0.0))
        zs = _sel_left(first, z)
        n_ref[h] = zs[:, 0:1] * n0_ref[h] + _sel_left(member, kw)
        m_ref[h] = jnp.broadcast_to(zs[:, 1:2], (NB, LANES))
        for b in range(NB):
            wd, par = b // 2, b % 2
            valid = upper if par == 0 else jnp.logical_not(upper)
            vwin = jnp.where(valid, vf[WIN * wd:WIN * (wd + 1)], 0.0).astype(BF16)
            kwin = kw[WIN * wd:WIN * (wd + 1)].astype(BF16)
            c_ref[0, b, h] = zs[b:b + 1, 0:1] * c0_ref[0, b, h] + _tn(vwin, kwin)


def _sample_grid(n_seq, depth, layer, has_prev):
    nblk = n_seq // SAMPLE_NB
    slots = 1 if has_prev else depth
    blk = lambda s, i: jnp.where(s == 0, i, nblk - 1)
    lay = lambda s: (layer + s) % depth
    return (slots, nblk), blk, lay


def _mlstm_sample(p, gates, bias, gnorm, c0, n0_hm, n_rep, m_rep, c_prev, *, layer, n_seq):
    nbk = SAMPLE_NB
    rb = nbk * TS
    has_prev = c_prev is not None
    grid, blk, lay = _sample_grid(n_seq, c0.shape[0], layer, has_prev)
    rows = lambda width, col: pl.BlockSpec((rb, width), lambda s, i, col=col: (blk(s, i), col))
    vec = lambda width: pl.BlockSpec((1, width), lambda s, i: (0, 0))
    hm = pl.BlockSpec((HEADS, nbk, LANES), lambda s, i: (0, blk(s, i), 0))
    in_specs = [rows(512, C512_MQ), rows(512, C512_MK), rows(1024, C1K_MV), rows(LANES, 0), vec(LANES), vec(D_MODEL),
                pl.BlockSpec((1, nbk, HEADS, DV, DK), lambda s, i: (layer, blk(s, i), 0, 0, 0)),
                hm, rows(HEADS * DK, 0), rows(LANES, 0)]
    args = [p, p, p, gates, bias, gnorm, c0, n0_hm, n_rep, m_rep]
    aliases = {}
    if has_prev:
        in_specs.append(pl.BlockSpec(memory_space=pl.ANY))
        args.append(c_prev)
        aliases[len(args) - 1] = 1
    return pl.pallas_call(
        functools.partial(_mlstm_sample_body, NB=nbk),
        grid=grid,
        in_specs=in_specs,
        out_specs=(rows(D_MODEL, 0),
                   pl.BlockSpec((1, nbk, HEADS, DV, DK), lambda s, i: (lay(s), i, 0, 0, 0)),
                   hm, hm),
        out_shape=(jax.ShapeDtypeStruct((n_seq * TS, D_MODEL), BF16),
                   jax.ShapeDtypeStruct(c0.shape, F32),
                   jax.ShapeDtypeStruct((HEADS, n_seq, LANES), F32),
                   jax.ShapeDtypeStruct((HEADS, n_seq, LANES), F32)),
        input_output_aliases=aliases,
        compiler_params=_cparams(2, 56), name="mlstm_sample",
    )(*args)


def _ref_rows(b, m, L):
    if m >= SUBLANES:
        nb = L // m
        b3 = b.reshape(nb, m, LANES)
        r = b3[:, m // 2 - 1:m // 2, :]
        return jnp.broadcast_to(r, (nb, m, LANES)).reshape(L, LANES)
    bt = b.reshape(L // SUBLANES, SUBLANES, LANES)
    sub = lax.broadcasted_iota(jnp.int32, bt.shape, 1)
    out = None
    for j in range(SUBLANES // m):
        idx = j * m + m // 2 - 1
        rj = jnp.broadcast_to(bt[:, idx:idx + 1, :], bt.shape)
        out = rj if out is None else jnp.where(sub >= j * m, rj, out)
    return out.reshape(L, LANES)


def _pair_level(L):
    rowi = lax.broadcasted_iota(jnp.int32, (L, L), 0)
    coli = lax.broadcasted_iota(jnp.int32, (L, L), 1)
    return jnp.where(rowi > coli, 31 - lax.clz(rowi ^ coli), -1)


def _gla_scores(qb, kb, b2, L, top, level):
    a_mat = jnp.zeros((L, L), F32)
    m = top
    while m >= 2:
        d = b2 - _ref_rows(b2, m, L)
        f = jnp.exp2(jnp.minimum(d, -d)).astype(BF16)
        am = _nt(qb * f, kb * f)
        a_mat = jnp.where(level == (m // 2).bit_length() - 1, am, a_mat)
        m //= 2
    return a_mat


def _gla_chunk(q, k, v, la, states, gnorm, L):
    rowi = lax.broadcasted_iota(jnp.int32, (L, L), 0)
    coli = lax.broadcasted_iota(jnp.int32, (L, L), 1)
    tril = _one_hot(rowi >= coli)
    level = _pair_level(L)
    scale = DK ** -0.5
    ys, new_states = [], []
    for h in range(HEADS):
        s_mat = states[h]
        qb = q[:, h * DK:(h + 1) * DK]
        kb = k[:, h * DK:(h + 1) * DK]
        qf = qb.astype(F32) * scale
        kf = kb.astype(F32)
        vh = v[:, h * DV:(h + 1) * DV]
        b2 = _sel_left2(tril, la[:, h * DK:(h + 1) * DK])
        a_mat = _gla_scores(qb, kb, b2, L, L, level) * scale
        diag = jnp.sum(qf * kf, axis=-1, keepdims=True)
        o = (_nn(a_mat.astype(BF16), vh) + diag * vh.astype(F32)
             + _nn((qf * jnp.exp2(b2)).astype(BF16), s_mat.astype(BF16)))
        ys.append(_head_rmsnorm(o, gnorm[:, h * DV:(h + 1) * DV]))
        b_last = b2[L - 1:L, :]
        kd = (kf * jnp.exp2(b_last - b2)).astype(BF16)
        e_col = jnp.broadcast_to(jnp.exp2(b_last), (SUBLANES, LANES)).T[:, 0:1]
        new_states.append(e_col * s_mat + _tn(kd, vh))
    return ys, new_states


def _gla_log_decay(gates, w2_ref, b2_ref):
    return _log_sigmoid(_nn(gates.astype(BF16), w2_ref[...]) + b2_ref[...]) * (LOG2E / GLA_TAU)


def _gla_prompt_body(q_ref, k_ref, v_ref, g_ref, w2_ref, b2_ref, gn_ref, y_ref, s_ref, *, L, NC):
    @pl.when(pl.program_id(1) == 0)
    def _():
        s_ref[...] = jnp.zeros_like(s_ref)

    la_all = _gla_log_decay(g_ref[...], w2_ref, b2_ref)
    gnorm = gn_ref[...]
    for c in range(NC):
        sl = slice(c * L, (c + 1) * L)
        states = [s_ref[0, h] for h in range(HEADS)]
        ys, new = _gla_chunk(q_ref[sl, :], k_ref[sl, :], v_ref[sl, :], la_all[sl], states, gnorm, L)
        for h in range(HEADS):
            y_ref[sl, h * DV:(h + 1) * DV] = ys[h].astype(BF16)
            s_ref[0, h] = new[h]


def _gla_prompt(p, gates, w2, b2, gnorm, *, nb, t_len):
    tb = MIX_TB
    nblk = t_len // tb
    rows = lambda width, col: pl.BlockSpec((tb, width), lambda b, t, col=col: (b * nblk + t, col))
    vec = lambda width: pl.BlockSpec((1, width), lambda b, t: (0, 0))
    return pl.pallas_call(
        functools.partial(_gla_prompt_body, L=GLA_CHUNK, NC=tb // GLA_CHUNK),
        grid=(nb, nblk),
        in_specs=[rows(512, C512_GQ), rows(512, C512_GK), rows(1024, C1K_GV), rows(LANES, 0),
                  pl.BlockSpec((LANES, 512), lambda b, t: (0, 0)), vec(512), vec(D_MODEL)],
        out_specs=(rows(D_MODEL, 0), pl.BlockSpec((1, HEADS, DK, DV), lambda b, t: (b, 0, 0, 0))),
        out_shape=(jax.ShapeDtypeStruct((nb * t_len, D_MODEL), BF16), jax.ShapeDtypeStruct((nb, HEADS, DK, DV), F32)),
        compiler_params=_cparams(2, 40), name="gla_prompt",
    )(p, p, p, gates, w2, b2, gnorm)


def _gla_sample_body(*refs, NB):
    s_ref = refs[-1]

    @pl.when(pl.program_id(0) == 0)
    def _():
        _gla_sample_compute(*refs, NB=NB)

    @pl.when(pl.program_id(0) != 0)
    def _():
        s_ref[...] = jnp.zeros_like(s_ref)


def _gla_sample_compute(q_ref, k_ref, v_ref, g_ref, w2_ref, b2_ref, gn_ref, s0_ref, *rest, NB):
    y_ref, s_ref = rest[-2:]
    L = NB * TS
    rowi, coli, same, first, _ = _seq_masks(L, NB)
    tril_seq = _one_hot(jnp.logical_and(same, rowi >= coli))
    full_seq = _one_hot(same)
    level = _pair_level(L)
    la_all = _gla_log_decay(g_ref[...], w2_ref, b2_ref)
    gnorm = gn_ref[...]
    q_all = q_ref[...].astype(F32)
    k_all = k_ref[...].astype(F32)
    v_all = v_ref[...].astype(F32)
    upper = lax.broadcasted_iota(jnp.int32, (WIN, 1), 0) < TS
    scale = DK ** -0.5
    for h in range(HEADS):
        qb = q_ref[:, h * DK:(h + 1) * DK]
        kb = k_ref[:, h * DK:(h + 1) * DK]
        qf = q_all[:, h * DK:(h + 1) * DK] * scale
        kf = k_all[:, h * DK:(h + 1) * DK]
        vf = v_all[:, h * DV:(h + 1) * DV]
        vh = vf.astype(BF16)
        la = la_all[:, h * DK:(h + 1) * DK]
        b2 = _sel_left2(tril_seq, la)
        btot = _sel_left2(full_seq, la)
        a_mat = _gla_scores(qb, kb, b2, L, TS, level) * scale
        diag = jnp.sum(qf * kf, axis=-1, keepdims=True)
        qe = qf * jnp.exp2(b2)
        inter = []
        for wd in range(NB // 2):
            qw = qe[WIN * wd:WIN * (wd + 1)].astype(BF16)
            o0 = _nn(qw, s0_ref[0, 2 * wd, h].astype(BF16))
            o1 = _nn(qw, s0_ref[0, 2 * wd + 1, h].astype(BF16))
            inter.append(jnp.where(upper, o0, o1))
        o = _nn(a_mat.astype(BF16), vh) + diag * vf + jnp.concatenate(inter, axis=0)
        y_ref[:, h * DV:(h + 1) * DV] = _head_rmsnorm(o, gnorm[:, h * DV:(h + 1) * DV]).astype(BF16)
        kd = kf * jnp.exp2(btot - b2)
        e_cols = jnp.exp2(_sel_left(first, btot)).T
        for bq in range(NB):
            wd, par = bq // 2, bq % 2
            valid = upper if par == 0 else jnp.logical_not(upper)
            kwin = jnp.where(valid, kd[WIN * wd:WIN * (wd + 1)], 0.0).astype(BF16)
            vwin = vf[WIN * wd:WIN * (wd + 1)].astype(BF16)
            s_ref[0, bq, h] = e_cols[:, bq:bq + 1] * s0_ref[0, bq, h] + _tn(kwin, vwin)


def _gla_sample(p, gates, w2, b2, gnorm, s0, s_prev, *, layer, n_seq):
    nbk = SAMPLE_NB
    rb = nbk * TS
    has_prev = s_prev is not None
    grid, blk, lay = _sample_grid(n_seq, s0.shape[0], layer, has_prev)
    rows = lambda width, col: pl.BlockSpec((rb, width), lambda s, i, col=col: (blk(s, i), col))
    vec = lambda width: pl.BlockSpec((1, width), lambda s, i: (0, 0))
    in_specs = [rows(512, C512_GQ), rows(512, C512_GK), rows(1024, C1K_GV), rows(LANES, 0),
                pl.BlockSpec((LANES, 512), lambda s, i: (0, 0)), vec(512), vec(D_MODEL),
                pl.BlockSpec((1, nbk, HEADS, DK, DV), lambda s, i: (layer, blk(s, i), 0, 0, 0))]
    args = [p, p, p, gates, w2, b2, gnorm, s0]
    aliases = {}
    if has_prev:
        in_specs.append(pl.BlockSpec(memory_space=pl.ANY))
        args.append(s_prev)
        aliases[len(args) - 1] = 1
    return pl.pallas_call(
        functools.partial(_gla_sample_body, NB=nbk),
        grid=grid,
        in_specs=in_specs,
        out_specs=(rows(D_MODEL, 0), pl.BlockSpec((1, nbk, HEADS, DK, DV), lambda s, i: (lay(s), i, 0, 0, 0))),
        out_shape=(jax.ShapeDtypeStruct((n_seq * TS, D_MODEL), BF16), jax.ShapeDtypeStruct(s0.shape, F32)),
        input_output_aliases=aliases,
        compiler_params=_cparams(2, 56), name="gla_sample",
    )(*args)


def _lru_gates(xc, wa_ref, ba_ref, wi_ref, bi_ref, lam_ref):
    ra, ri = [], []
    for nb in range(LRU_BLOCKS):
        xs = xc[:, nb * LRU_BW:(nb + 1) * LRU_BW].astype(BF16)
        ra.append(_nn(xs, wa_ref[nb]))
        ri.append(_nn(xs, wi_ref[nb]))
    r = _sigmoid(jnp.concatenate(ra, axis=-1) + ba_ref[...])
    ig = _sigmoid(jnp.concatenate(ri, axis=-1) + bi_ref[...])
    a = jnp.exp(((-LRU_C) * _softplus(-lam_ref[...])) * r)
    om = 1.0 - a * a
    mult = om * lax.rsqrt(jnp.maximum(om, 1e-30))
    return a, ig * xc, mult


def _conv_shift_matrices(tb):
    nsh = CONV_W - 1
    j = jnp.arange(nsh)[:, None, None] + 1
    t = jnp.arange(tb)[None, :, None]
    s = jnp.arange(tb)[None, None, :]
    shift = (t - s == j).astype(BF16).reshape(nsh * tb, tb)
    t8 = jnp.arange(SUBLANES)[None, :, None]
    s8 = jnp.arange(SUBLANES)[None, None, :]
    tail = (s8 - t8 == SUBLANES - j).astype(BF16).reshape(nsh * SUBLANES, SUBLANES)
    return shift, tail


def _lru_prompt_body(x_ref, shift_ref, tail_ref, wc_ref, bc_ref, wa_ref, ba_ref, wi_ref, bi_ref, lam_ref,
                     h_ref, hfin_ref, cfin_ref, xprev, hcar, *, TB):
    t = pl.program_id(1)

    @pl.when(t == 0)
    def _():
        xprev[...] = jnp.zeros_like(xprev)
        hcar[...] = jnp.zeros_like(hcar)

    xb = x_ref[...]
    x = xb.astype(F32)
    nsh = CONV_W - 1
    shifted = _nn(shift_ref[...], xb)
    from_prev = _nn(tail_ref[...], xprev[...].astype(BF16))
    wc = wc_ref[...]
    xc = bc_ref[...] + wc[CONV_W - 1:CONV_W, :] * x
    for j in range(nsh):
        sj = shifted[j * TB:(j + 1) * TB]
        sj = jnp.concatenate([sj[0:SUBLANES] + from_prev[j * SUBLANES:(j + 1) * SUBLANES], sj[SUBLANES:]], axis=0)
        xc = xc + wc[CONV_W - 2 - j:CONV_W - 1 - j, :] * sj
    xprev[...] = x[TB - SUBLANES:TB, :]

    a, gx, mult = _lru_gates(xc, wa_ref, ba_ref, wi_ref, bi_ref, lam_ref)
    row8 = lax.broadcasted_iota(jnp.int32, (SUBLANES, 1), 0)
    m_first = jnp.where(row8 + t == 0, 1.0, mult[0:SUBLANES])
    u = jnp.concatenate([m_first, mult[SUBLANES:]], axis=0) * gx
    nt8 = TB // SUBLANES
    a3 = a.reshape(nt8, SUBLANES, D_MODEL)
    u3 = u.reshape(nt8, SUBLANES, D_MODEL)
    sub = lax.broadcasted_iota(jnp.int32, (1, SUBLANES, D_MODEL), 1)
    d = 1
    while d < SUBLANES:
        keep = sub >= d
        a_sh = jnp.where(keep, pltpu.roll(a3, d, 1), 1.0)
        u_sh = jnp.where(keep, pltpu.roll(u3, d, 1), 0.0)
        u3 = a3 * u_sh + u3
        a3 = a3 * a_sh
        d *= 2
    h_in = hcar[...]
    hs = []
    for j in range(nt8):
        hj = a3[j] * h_in + u3[j]
        hs.append(hj)
        h_in = hj[SUBLANES - 1:SUBLANES]
    h_ref[...] = jnp.concatenate(hs, axis=0).astype(BF16)
    hcar[...] = h_in
    hfin_ref[0] = h_in
    cfin_ref[0] = x[TB - (CONV_W - 1):TB, :]


def _lru_prompt(p, wc, bc, wa, ba, wi, bi, lam, *, nb, t_len):
    tb = LRU_TB
    nblk = t_len // tb
    vec = pl.BlockSpec((1, D_MODEL), lambda b, t: (0, 0))
    wblk = pl.BlockSpec((LRU_BLOCKS, LRU_BW, LRU_BW), lambda b, t: (0, 0, 0))
    shift, tail = _conv_shift_matrices(tb)
    whole = lambda arr: pl.BlockSpec(arr.shape, lambda b, t: (0, 0))
    return pl.pallas_call(
        functools.partial(_lru_prompt_body, TB=tb),
        grid=(nb, nblk),
        in_specs=[pl.BlockSpec((tb, D_MODEL), lambda b, t: (b * nblk + t, C1K_LX)), whole(shift), whole(tail),
                  pl.BlockSpec((CONV_W, D_MODEL), lambda b, t: (0, 0)), vec, wblk, vec, wblk, vec, vec],
        out_specs=(pl.BlockSpec((tb, D_MODEL), lambda b, t: (b * nblk + t, 0)),
                   pl.BlockSpec((1, 1, D_MODEL), lambda b, t: (b, 0, 0)),
                   pl.BlockSpec((1, CONV_W - 1, D_MODEL), lambda b, t: (b, 0, 0))),
        out_shape=(jax.ShapeDtypeStruct((nb * t_len, D_MODEL), BF16),
                   jax.ShapeDtypeStruct((nb, 1, D_MODEL), F32),
                   jax.ShapeDtypeStruct((nb, CONV_W - 1, D_MODEL), F32)),
        scratch_shapes=[pltpu.VMEM((SUBLANES, D_MODEL), F32), pltpu.VMEM((1, D_MODEL), F32)],
        compiler_params=_cparams(2, 40), name="lru_prompt",
    )(p, shift, tail, wc, bc, wa, ba, wi, bi, lam)


def _lru_sample_body(x_ref, buf_ref, h0_ref, wc_ref, bc_ref, wa_ref, ba_ref, wi_ref, bi_ref, lam_ref,
                     h_ref, hfin_ref, cfin_ref, *, T):
    wc = wc_ref[...]
    xs = [buf_ref[j] for j in range(CONV_W - 1)] + [x_ref[t].astype(F32) for t in range(T)]
    h = h0_ref[...]
    for t in range(T):
        xc = bc_ref[...]
        for j in range(CONV_W):
            xc = xc + wc[j:j + 1, :] * xs[t + j]
        a, gx, mult = _lru_gates(xc, wa_ref, ba_ref, wi_ref, bi_ref, lam_ref)
        h = a * h + mult * gx
        h_ref[t] = h.astype(BF16)
    hfin_ref[...] = h
    for j in range(CONV_W - 1):
        cfin_ref[j] = xs[T + j]


def _lru_sample(x_tm, buf_tm, h0, wc, bc, wa, ba, wi, bi, lam):
    t_len, n_seq, _ = x_tm.shape
    return pl.pallas_call(
        functools.partial(_lru_sample_body, T=t_len),
        out_shape=(jax.ShapeDtypeStruct((t_len, n_seq, D_MODEL), BF16),
                   jax.ShapeDtypeStruct((n_seq, D_MODEL), F32),
                   jax.ShapeDtypeStruct((CONV_W - 1, n_seq, D_MODEL), F32)),
        compiler_params=pltpu.CompilerParams(vmem_limit_bytes=40 * 1024 * 1024), name="lru_sample",
    )(x_tm, buf_tm, h0, wc, bc, wa, ba, wi, bi, lam)


def kernel(x_prompt, x_sample, state_mlstm_C, state_mlstm_n, state_mlstm_m, state_gla_S, state_lru_h, state_lru_conv,
           norm_ffn1, w_ffn1_in, w_ffn1_out, norm_mix, w_in, b_ml_if, g_ml_norm, w_gla_lr2, b_gla_gate, g_gla_norm,
           w_conv, b_conv, w_lru_a, b_lru_a, w_lru_i, b_lru_i, lru_lambda, w_out, norm_ffn2, w_ffn2_in, w_ffn2_out,
           norm_final):
    nb, t_len, d = x_prompt.shape
    n_seq, ts, _ = x_sample.shape
    depth = w_in.shape[0]
    assert d == D_MODEL and ts == TS and t_len % MIX_TB == 0 and n_seq % SAMPLE_NB == 0

    xp = x_prompt.reshape(nb * t_len, d)
    xs = x_sample.reshape(n_seq * ts, d)

    f1_in, f1_out = w_ffn1_in.astype(BF16), w_ffn1_out.astype(BF16)
    f2_in, f2_out = w_ffn2_in.astype(BF16), w_ffn2_out.astype(BF16)
    w_o = w_out.astype(BF16)
    new_p = {k: [] for k in ("C", "n", "m", "S", "h", "conv")}
    new_s = {k: [] for k in ("n", "m", "h", "conv")}
    s_c_all, s_s_all = None, None
    yp = ys = None
    for l in range(depth):
        bias_if = jnp.concatenate([b_ml_if[l], jnp.zeros((LANES - 2 * HEADS,), F32)]).reshape(1, LANES)
        w2 = jnp.zeros((LANES, HEADS * DK), F32).at[2 * HEADS:2 * HEADS + GLA_RANK].set(w_gla_lr2[l]).astype(BF16)
        b2 = b_gla_gate[l].reshape(1, HEADS * DK)
        g_ml = g_ml_norm[l].reshape(1, d)
        g_gla = g_gla_norm[l].reshape(1, d)
        wc, bc = w_conv[l], b_conv[l].reshape(1, d)
        wa, wi = w_lru_a[l].astype(BF16), w_lru_i[l].astype(BF16)
        ba, bi, lam = b_lru_a[l].reshape(1, d), b_lru_i[l].reshape(1, d), lru_lambda[l].reshape(1, d)
        last = l == depth - 1
        g_next = norm_final if last else norm_ffn1[l + 1]

        x1, xn, gates = _ffn(xp, norm_ffn1[l], f1_in, f1_out, norm_mix[l], layer=l, final=False, w_gate=w_in)
        p = _proj(xn, w_in, l)
        g_ml_rep = jnp.broadcast_to(g_ml_norm[l][:, None], (d, LANES))
        y_ml, p_c, p_n, p_m = _mlstm_prompt(p, gates, bias_if, g_ml_rep, nb=nb, t_len=t_len)
        y_gla, p_s = _gla_prompt(p, gates, w2, b2, g_gla, nb=nb, t_len=t_len)
        h_lru, p_h, p_conv = _lru_prompt(p, wc, bc, wa, ba, wi, bi, lam, nb=nb, t_len=t_len)
        x2 = _merge(x1, p, y_ml, y_gla, h_lru, w_o, l)
        if last:
            yp = _ffn(x2, norm_ffn2[l], f2_in, f2_out, g_next, layer=l, final=True)
        else:
            xp, _ = _ffn(x2, norm_ffn2[l], f2_in, f2_out, g_next, layer=l, final=False)

        x1, xn, gates = _ffn(xs, norm_ffn1[l], f1_in, f1_out, norm_mix[l], layer=l, final=False, w_gate=w_in)
        p = _proj(xn, w_in, l)
        n0_hm = state_mlstm_n[l].transpose(1, 0, 2)
        n_rep = jnp.repeat(state_mlstm_n[l].reshape(n_seq, HEADS * DK), ts, axis=0)
        m_rep = jnp.repeat(jnp.pad(state_mlstm_m[l], ((0, 0), (0, LANES - HEADS))), ts, axis=0)
        y_ml, s_c_all, s_n, s_m = _mlstm_sample(p, gates, bias_if, g_ml, state_mlstm_C, n0_hm, n_rep, m_rep,
                                                s_c_all, layer=l, n_seq=n_seq)
        y_gla, s_s_all = _gla_sample(p, gates, w2, b2, g_gla, state_gla_S, s_s_all, layer=l, n_seq=n_seq)
        lx_s = p[:, C1K_LX * 1024:(C1K_LX + 1) * 1024].reshape(n_seq, ts, d).transpose(1, 0, 2)
        hs_tm, s_h, s_conv_tm = _lru_sample(lx_s, state_lru_conv[l].transpose(1, 0, 2), state_lru_h[l],
                                            wc, bc, wa, ba, wi, bi, lam)
        h_lru = hs_tm.transpose(1, 0, 2).reshape(n_seq * ts, d)
        x2 = _merge(x1, p, y_ml, y_gla, h_lru, w_o, l)
        if last:
            ys = _ffn(x2, norm_ffn2[l], f2_in, f2_out, g_next, layer=l, final=True)
        else:
            xs, _ = _ffn(x2, norm_ffn2[l], f2_in, f2_out, g_next, layer=l, final=False)

        new_p["C"].append(p_c)
        new_p["n"].append(p_n)
        new_p["m"].append(p_m[:, :, 0])
        new_p["S"].append(p_s)
        new_p["h"].append(p_h[:, 0, :])
        new_p["conv"].append(p_conv)
        new_s["n"].append(s_n.transpose(1, 0, 2))
        new_s["m"].append(s_m[:, :, 0].T)
        new_s["h"].append(s_h)
        new_s["conv"].append(s_conv_tm.transpose(1, 0, 2))

    st = lambda arrs: jnp.stack(arrs)
    return (yp.reshape(nb, t_len, d), ys.reshape(n_seq, ts, d),
            st(new_p["C"]), st(new_p["n"]), st(new_p["m"]), st(new_p["S"]), st(new_p["h"]), st(new_p["conv"]),
            s_c_all, st(new_s["n"]), st(new_s["m"]), s_s_all, st(new_s["h"]), st(new_s["conv"]))
```

```python
import functools

import jax
import jax.numpy as jnp
from jax import lax
from jax.experimental import pallas as pl
from jax.experimental.pallas import tpu as pltpu

F32 = jnp.float32
BF16 = jnp.bfloat16

D_MODEL = 1024
D_FF = 2816
HEADS = 4
DK = 128
DV = 256
GLA_RANK = 16
GLA_TAU = 16.0
LRU_BLOCKS = 8
LRU_BW = 128
LRU_C = 8.0
CONV_W = 4
EPS = 1e-6
NEG = -1e30
LOG2E = 1.4426950408889634

LANES = 128
SUBLANES = 8
MXU_N = 256

P_COLS = 11264
C512_MQ, C512_MK, C512_GQ, C512_GK = 0, 1, 6, 7
C1K_MV, C1K_MO, C1K_GV, C1K_GG, C1K_LX, C1K_LY, C1K_MG0 = 1, 2, 4, 5, 6, 7, 8
W_IN_A = (0, 3072)
W_IN_IF = (3072, 3080)
W_IN_B = (3080, 6152)
W_IN_LR = (6152, 6168)
W_IN_C = (6168, 11288)
assert W_IN_IF[0] % 128 == 0 and W_IN_LR[0] % 128 == W_IN_IF[1] - W_IN_IF[0]
PROJ_TN = 1024
PROJ_WIN = PROJ_TN // 128 + 1
PROJ_REGIONS = ((0, 3, 0), (3, 6, W_IN_B[0] - W_IN_A[1]), (6, 11, W_IN_C[0] - W_IN_B[1] + W_IN_B[0] - W_IN_A[1]))

ML_CHUNK = 256
GLA_CHUNK = 128
MIX_TB = 512
LRU_TB = 256
TS = 4
TS_SHIFT = 2
SAMPLE_NB = 16
WIN = 8
FF_CHUNK = MXU_N


def _cparams(n_axes, vmem_mib):
    return pltpu.CompilerParams(dimension_semantics=("arbitrary",) * n_axes,
                                vmem_limit_bytes=vmem_mib * 1024 * 1024)


def _pick_tile(n, target, mult):
    best = None
    for t in range(mult, min(n, target) + 1, mult):
        if n % t == 0:
            best = t
    assert best is not None, (n, target, mult)
    return best


def _sigmoid(x):
    return 0.5 * jnp.tanh(0.5 * x) + 0.5


def _log_sigmoid(x):
    return jnp.minimum(x, 0.0) - jnp.log(1.0 + jnp.exp(-jnp.abs(x)))


def _softplus(x):
    return jnp.maximum(x, 0.0) + jnp.log(1.0 + jnp.exp(-jnp.abs(x)))


def _rms(x, g):
    return x * lax.rsqrt(jnp.mean(x * x, axis=-1, keepdims=True) + EPS) * g


def _nn(a, b):
    return jnp.dot(a, b, preferred_element_type=F32)


def _nt(a, b):
    return lax.dot_general(a, b, (((1,), (1,)), ((), ())), preferred_element_type=F32)


def _tn(a, b):
    return lax.dot_general(a, b, (((0,), (0,)), ((), ())), preferred_element_type=F32)


def _split3(x):
    hi = x.astype(BF16)
    r1 = x - hi.astype(F32)
    mid = r1.astype(BF16)
    lo = (r1 - mid.astype(F32)).astype(BF16)
    return hi, mid, lo


def _sel_left(sel, x):
    hi, mid, lo = _split3(x)
    return _nn(sel, hi) + _nn(sel, mid) + _nn(sel, lo)


def _sel_right(x, sel):
    hi, mid, lo = _split3(x)
    return _nn(hi, sel) + _nn(mid, sel) + _nn(lo, sel)


def _sel_left2(sel, x):
    hi = x.astype(BF16)
    lo = (x - hi.astype(F32)).astype(BF16)
    return _nn(sel, hi) + _nn(sel, lo)


def _one_hot(mask):
    return jnp.where(mask, 1.0, 0.0).astype(BF16)


def _head_layernorm(h, g):
    mu = jnp.mean(h, axis=-1, keepdims=True)
    hc = h - mu
    return hc * lax.rsqrt(jnp.mean(hc * hc, axis=-1, keepdims=True) + EPS) * g


def _head_rmsnorm(h, g):
    return h * lax.rsqrt(jnp.mean(h * h, axis=-1, keepdims=True) + EPS) * g


def _ffn_body(x_ref, g_ref, win_ref, wout_ref, gn_ref, *rest, n_chunks, final, with_gates):
    if with_gates:
        wif_ref, wlr_ref, rest = rest[0], rest[1], rest[2:]
    if final:
        y_ref, xn_s, acc_s = rest
    elif with_gates:
        y_ref, xn_ref, gates_ref, xn_s, acc_s = rest
    else:
        y_ref, xn_ref, xn_s, acc_s = rest
    xn_s[...] = _rms(x_ref[...], g_ref[...]).astype(BF16)
    for c in range(n_chunks):
        xn = xn_s[...]
        g = _nn(xn, win_ref[:, c * FF_CHUNK:(c + 1) * FF_CHUNK])
        u = _nn(xn, win_ref[:, D_FF + c * FF_CHUNK:D_FF + (c + 1) * FF_CHUNK])
        t = jnp.tanh(0.5 * g)
        a = (g * (t + 1.0)) * u
        part = _nn(a.astype(BF16), wout_ref[c * FF_CHUNK:(c + 1) * FF_CHUNK, :])
        if c == 0:
            acc_s[...] = part
        else:
            acc_s[...] += part
    xo = x_ref[...] + 0.25 * acc_s[...]
    if final:
        y_ref[...] = _rms(xo, gn_ref[...])
    else:
        y_ref[...] = xo
        xn_next = _rms(xo, gn_ref[...]).astype(BF16)
        xn_ref[...] = xn_next
        if with_gates:
            w_pair = jnp.concatenate([wif_ref[...], wlr_ref[...]], axis=1).astype(BF16)
            g_pair = _nn(xn_next, w_pair)
            lane = lax.broadcasted_iota(jnp.int32, (g_pair.shape[0], LANES), 1)
            gates_ref[...] = jnp.where(lane < 2 * HEADS, g_pair[:, 0:LANES], g_pair[:, LANES:2 * LANES])


def _ffn(x, g, w_in_c, w_out_c, g_next, *, layer, final, w_gate=None):
    rows = x.shape[0]
    tm = _pick_tile(rows, 1024, 16)
    n_chunks = D_FF // FF_CHUNK
    with_gates = w_gate is not None
    row_spec = pl.BlockSpec((tm, D_MODEL), lambda i: (i, 0))
    vec = pl.BlockSpec((1, D_MODEL), lambda i: (0, 0))
    of_layer = lambda arr: pl.BlockSpec((None,) + arr.shape[1:], lambda i: (layer, 0, 0), pipeline_mode=pl.Buffered(1))
    in_specs = [row_spec, vec, of_layer(w_in_c), of_layer(w_out_c), vec]
    args = [x, g.reshape(1, D_MODEL), w_in_c, w_out_c, g_next.reshape(1, D_MODEL)]
    if final:
        out_shape = jax.ShapeDtypeStruct((rows, D_MODEL), F32)
        out_specs = row_spec
    else:
        out_shape = [jax.ShapeDtypeStruct((rows, D_MODEL), F32), jax.ShapeDtypeStruct((rows, D_MODEL), BF16)]
        out_specs = [row_spec, row_spec]
        if with_gates:
            for col in (W_IN_IF[0], W_IN_LR[0]):
                in_specs.append(pl.BlockSpec((None, D_MODEL, LANES), lambda i, col=col: (layer, 0, col // LANES),
                                             pipeline_mode=pl.Buffered(1)))
                args.append(w_gate)
            out_shape.append(jax.ShapeDtypeStruct((rows, LANES), F32))
            out_specs.append(pl.BlockSpec((tm, LANES), lambda i: (i, 0)))
    return pl.pallas_call(
        functools.partial(_ffn_body, n_chunks=n_chunks, final=final, with_gates=with_gates),
        grid=(rows // tm,),
        in_specs=in_specs, out_specs=out_specs, out_shape=out_shape,
        scratch_shapes=[pltpu.VMEM((tm, D_MODEL), BF16), pltpu.VMEM((tm, D_MODEL), F32)],
        compiler_params=_cparams(1, 56), name="ffn_final" if final else "ffn",
    )(*args)


def _proj_body(xn_ref, *rest):
    w_refs, p_ref, w_s = rest[:PROJ_WIN], rest[PROJ_WIN], rest[PROJ_WIN + 1]
    j = pl.program_id(0)

    def build(shift):
        cat = jnp.concatenate([r[...].astype(F32) for r in w_refs], axis=1)
        w_s[...] = cat[:, shift:shift + PROJ_TN].astype(BF16)

    first_row_tile = pl.program_id(1) == 0
    for lo, hi, shift in PROJ_REGIONS:
        pl.when(jnp.logical_and(first_row_tile, jnp.logical_and(j >= lo, j < hi)))(functools.partial(build, shift))
    p_ref[...] = _nn(xn_ref[...], w_s[...]).astype(BF16)


def _proj(xn, w_in_all, layer):
    rows = xn.shape[0]
    tm = _pick_tile(rows, 2048, 16)
    wblk = lambda k: pl.BlockSpec((None, D_MODEL, LANES), lambda j, i, k=k: (layer, 0, j * (PROJ_TN // LANES) + k))
    return pl.pallas_call(
        _proj_body, grid=(P_COLS // PROJ_TN, rows // tm),
        in_specs=[pl.BlockSpec((tm, D_MODEL), lambda j, i: (i, 0))] + [wblk(k) for k in range(PROJ_WIN)],
        out_specs=pl.BlockSpec((tm, PROJ_TN), lambda j, i: (i, j)),
        out_shape=jax.ShapeDtypeStruct((rows, P_COLS), BF16),
        scratch_shapes=[pltpu.VMEM((D_MODEL, PROJ_TN), BF16)],
        compiler_params=_cparams(2, 48), name="proj",
    )(xn, *([w_in_all] * PROJ_WIN))


def _merge_body(x_ref, mo_ref, gg_ref, ly_ref, m0_ref, m1_ref, m2_ref, yml_ref, ygla_ref, hlru_ref, w_ref, o_ref):
    f = lambda r: r[...].astype(F32)
    th = lambda r: jnp.tanh(0.5 * f(r)) + 1.0
    gg = f(gg_ref)
    ly = f(ly_ref)
    y_ml = f(yml_ref) * th(mo_ref)
    y_gla = f(ygla_ref) * (gg * th(gg_ref))
    y_lru = f(hlru_ref) * (ly * (1.0 + jnp.tanh(0.7978845608028654 * (ly + 0.044715 * (ly * ly * ly)))))
    merged4 = th(m0_ref) * y_ml + th(m1_ref) * y_gla + th(m2_ref) * y_lru
    o_ref[...] = x_ref[...] + _nn((0.25 * merged4).astype(BF16), w_ref[...])


def _merge(x, p, y_ml, y_gla, h_lru, w_out, layer):
    rows = x.shape[0]
    tm = _pick_tile(rows, 512, 16)
    row = lambda c: pl.BlockSpec((tm, D_MODEL), lambda i, c=c: (i, c))
    return pl.pallas_call(
        _merge_body, grid=(rows // tm,),
        in_specs=[row(0), row(C1K_MO), row(C1K_GG), row(C1K_LY), row(C1K_MG0), row(C1K_MG0 + 1), row(C1K_MG0 + 2),
                  row(0), row(0), row(0), pl.BlockSpec((None, D_MODEL, D_MODEL), lambda i: (layer, 0, 0))],
        out_specs=row(0),
        out_shape=jax.ShapeDtypeStruct((rows, D_MODEL), F32),
        compiler_params=_cparams(1, 56), name="merge",
    )(x, p, p, p, p, p, p, y_ml, y_gla, h_lru, w_out)


def _mlstm_chunk(q, k, v, gi, lf, states, g_rep, L):
    rowi = lax.broadcasted_iota(jnp.int32, (L, L), 0)
    coli = lax.broadcasted_iota(jnp.int32, (L, L), 1)
    upper = rowi <= coli
    b_cols = _sel_left(_one_hot(rowi >= coli), lf)
    lf_t = lf.T[0:SUBLANES]
    b_rows = _sel_right(lf_t, _one_hot(upper))
    scale = DK ** -0.5
    ys, new_states = [], []
    for h in range(HEADS):
        c_mat, n_vec, m_prev = states[h]
        qh = q[:, h * DK:(h + 1) * DK]
        kh = k[:, h * DK:(h + 1) * DK]
        vh = v[:, h * DV:(h + 1) * DV]
        b_col = b_cols[:, HEADS + h:HEADS + h + 1]
        b_row = b_rows[HEADS + h:HEADS + h + 1, :]
        c_col = gi[:, h:h + 1] - b_col
        dm = jnp.where(upper, c_col, NEG)
        big_m = jnp.maximum(m_prev, jnp.max(dm, axis=0, keepdims=True))
        w = (_nt(kh, qh) * scale) * jnp.exp(dm - big_m)
        s_inter = jnp.exp(m_prev - big_m)
        num = _tn(vh, w.astype(BF16)) + s_inter * _nt(c_mat.astype(BF16), qh)
        qn = _nt(jnp.broadcast_to(n_vec, (SUBLANES, DK)).astype(BF16), qh)[0:1, :]
        den = jnp.sum(w, axis=0, keepdims=True) + s_inter * qn
        hh = num / jnp.maximum(jnp.abs(den), jnp.exp(-(b_row + big_m)))
        mu = jnp.mean(hh, axis=0, keepdims=True)
        hc = hh - mu
        g_h = jnp.concatenate([g_rep[h * DV:(h + 1) * DV]] * (L // LANES), axis=1)
        y_t = hc * lax.rsqrt(jnp.mean(hc * hc, axis=0, keepdims=True) + EPS) * g_h
        ys.append(y_t.astype(BF16).T)
        m_last = big_m[:, L - 1:L]
        wk = jnp.exp(c_col - m_last)
        decay = jnp.exp(m_prev - m_last)
        kw = kh.astype(F32) * (wk * scale)
        c_new = decay * c_mat + _tn(vh, kw.astype(BF16))
        n_new = decay * n_vec + jnp.sum(kw, axis=0, keepdims=True)
        m_new = b_col[L - 1:L, :] + m_last
        new_states.append((c_new, n_new, m_new))
    return ys, new_states


def _mlstm_prompt_body(q_ref, k_ref, v_ref, g_ref, bias_ref, gn_ref, y_ref, c_ref, n_ref, m_ref, *, L, NC):
    @pl.when(pl.program_id(1) == 0)
    def _():
        c_ref[...] = jnp.zeros_like(c_ref)
        n_ref[...] = jnp.zeros_like(n_ref)
        m_ref[...] = jnp.zeros_like(m_ref)

    gates = g_ref[...] + bias_ref[...]
    lf_all = _log_sigmoid(gates)
    gnorm = gn_ref[...]
    for c in range(NC):
        sl = slice(c * L, (c + 1) * L)
        states = [(c_ref[0, h], n_ref[0, h:h + 1, :], m_ref[0, h:h + 1, 0:1]) for h in range(HEADS)]
        ys, new = _mlstm_chunk(q_ref[sl, :], k_ref[sl, :], v_ref[sl, :], gates[sl], lf_all[sl], states, gnorm, L)
        for h in range(HEADS):
            y_ref[sl, h * DV:(h + 1) * DV] = ys[h]
            c_ref[0, h] = new[h][0]
            n_ref[0, h:h + 1, :] = new[h][1]
            m_ref[0, h:h + 1, :] = jnp.broadcast_to(new[h][2], (1, LANES))


def _mlstm_prompt(p, gates, bias, g_rep, *, nb, t_len):
    tb = MIX_TB
    nblk = t_len // tb
    rows = lambda width, col: pl.BlockSpec((tb, width), lambda b, t, col=col: (b * nblk + t, col))
    vec = lambda width: pl.BlockSpec((1, width), lambda b, t: (0, 0))
    return pl.pallas_call(
        functools.partial(_mlstm_prompt_body, L=ML_CHUNK, NC=tb // ML_CHUNK),
        grid=(nb, nblk),
        in_specs=[rows(512, C512_MQ), rows(512, C512_MK), rows(1024, C1K_MV), rows(LANES, 0), vec(LANES),
                  pl.BlockSpec((D_MODEL, LANES), lambda b, t: (0, 0))],
        out_specs=(rows(D_MODEL, 0),
                   pl.BlockSpec((1, HEADS, DV, DK), lambda b, t: (b, 0, 0, 0)),
                   pl.BlockSpec((1, HEADS, DK), lambda b, t: (b, 0, 0)),
                   pl.BlockSpec((1, HEADS, LANES), lambda b, t: (b, 0, 0))),
        out_shape=(jax.ShapeDtypeStruct((nb * t_len, D_MODEL), BF16),
                   jax.ShapeDtypeStruct((nb, HEADS, DV, DK), F32),
                   jax.ShapeDtypeStruct((nb, HEADS, DK), F32),
                   jax.ShapeDtypeStruct((nb, HEADS, LANES), F32)),
        compiler_params=_cparams(2, 40), name="mlstm_prompt",
    )(p, p, p, gates, bias, g_rep)


def _seq_masks(L, nseq):
    rowi = lax.broadcasted_iota(jnp.int32, (L, L), 0)
    coli = lax.broadcasted_iota(jnp.int32, (L, L), 1)
    same = (rowi >> TS_SHIFT) == (coli >> TS_SHIFT)
    sb = lax.broadcasted_iota(jnp.int32, (nseq, L), 0)
    sr = lax.broadcasted_iota(jnp.int32, (nseq, L), 1)
    first = _one_hot(sr == sb * TS)
    member = _one_hot((sr >> TS_SHIFT) == sb)
    return rowi, coli, same, first, member


def _mlstm_sample_body(*refs, NB):
    c_ref = refs[-3]

    @pl.when(pl.program_id(0) == 0)
    def _():
        _mlstm_sample_compute(*refs, NB=NB)

    @pl.when(pl.program_id(0) != 0)
    def _():
        c_ref[...] = jnp.zeros_like(c_ref)


def _mlstm_sample_compute(q_ref, k_ref, v_ref, g_ref, bias_ref, gn_ref, c0_ref, n0_ref, nrep_ref, mrep_ref, *rest, NB):
    y_ref, c_ref, n_ref, m_ref = rest[-4:]
    L = NB * TS
    rowi, coli, same, first, member = _seq_masks(L, NB)
    causal = jnp.logical_and(same, rowi >= coli)
    gates = g_ref[...] + bias_ref[...]
    lf = _log_sigmoid(gates)
    b_cols = _sel_left(_one_hot(causal), lf)
    btot_cols = _sel_left(_one_hot(same), lf)
    gi_t = gates.T[0:SUBLANES]
    lf_t = lf.T[0:SUBLANES]
    b_rows = _sel_right(lf_t, _one_hot(jnp.logical_and(same, rowi <= coli)))
    gnorm = gn_ref[...]
    q_all = q_ref[...].astype(F32)
    k_all = k_ref[...].astype(F32)
    v_all = v_ref[...].astype(F32)
    m_rep = mrep_ref[...]
    lane = lax.broadcasted_iota(jnp.int32, (L, LANES), 1)
    upper = lax.broadcasted_iota(jnp.int32, (WIN, 1), 0) < TS
    scale = DK ** -0.5
    for h in range(HEADS):
        qf = q_all[:, h * DK:(h + 1) * DK]
        kf = k_all[:, h * DK:(h + 1) * DK]
        vf = v_all[:, h * DV:(h + 1) * DV]
        qh, kh, vh = qf.astype(BF16), kf.astype(BF16), vf.astype(BF16)
        i_col = gates[:, h:h + 1]
        b_col = b_cols[:, HEADS + h:HEADS + h + 1]
        c_row = gi_t[h:h + 1, :] - b_rows[HEADS + h:HEADS + h + 1, :]
        m_prev = m_rep[:, h:h + 1]
        dm = jnp.where(causal, c_row, NEG)
        big_m = jnp.maximum(m_prev, jnp.max(dm, axis=-1, keepdims=True))
        m_last = jnp.maximum(m_prev, jnp.max(jnp.where(same, c_row, NEG), axis=-1, keepdims=True))
        pmat = jnp.exp(dm - big_m)
        w = (_nt(qh, kh) * scale) * pmat
        s_inter = jnp.exp(m_prev - big_m)
        qc = []
        for wd in range(NB // 2):
            qw = qf[WIN * wd:WIN * (wd + 1)].astype(BF16)
            o0 = _nt(qw, c0_ref[0, 2 * wd, h].astype(BF16))
            o1 = _nt(qw, c0_ref[0, 2 * wd + 1, h].astype(BF16))
            qc.append(jnp.where(upper, o0, o1))
        num = _nn(w.astype(BF16), vh) + s_inter * jnp.concatenate(qc, axis=0)
        qn = jnp.sum(qf * nrep_ref[:, h * DK:(h + 1) * DK], axis=-1, keepdims=True)
        den = jnp.sum(w, axis=-1, keepdims=True) + s_inter * qn
        hh = num / jnp.maximum(jnp.abs(den), jnp.exp(-(b_col + big_m)))
        y_ref[:, h * DV:(h + 1) * DV] = _head_layernorm(hh, gnorm[:, h * DV:(h + 1) * DV]).astype(BF16)
        wk = jnp.exp((i_col - b_col) - m_last)
        decay_col = jnp.exp(m_prev - m_last)
        kw = kf * (wk * scale)
        m_new_col = btot_cols[:, HEADS + h:HEADS + h + 1] + m_last
        z = jnp.where(lane == 0, decay_col, jnp.where(lane == 1, m_new_col, 0.0))
        zs = _sel_left(first, z)
        n_ref[h] = zs[:, 0:1] * n0_ref[h] + _sel_left(member, kw)
        m_ref[h] = jnp.broadcast_to(zs[:, 1:2], (NB, LANES))
        for b in range(NB):
            wd, par = b // 2, b % 2
            valid = upper if par == 0 else jnp.logical_not(upper)
            vwin = jnp.where(valid, vf[WIN * wd:WIN * (wd + 1)], 0.0).astype(BF16)
            kwin = kw[WIN * wd:WIN * (wd + 1)].astype(BF16)
            c_ref[0, b, h] = zs[b:b + 1, 0:1] * c0_ref[0, b, h] + _tn(vwin, kwin)


def _sample_grid(n_seq, depth, layer, has_prev):
    nblk = n_seq // SAMPLE_NB
    slots = 1 if has_prev else depth
    blk = lambda s, i: jnp.where(s == 0, i, nblk - 1)
    lay = lambda s: (layer + s) % depth
    return (slots, nblk), blk, lay


def _mlstm_sample(p, gates, bias, gnorm, c0, n0_hm, n_rep, m_rep, c_prev, *, layer, n_seq):
    nbk = SAMPLE_NB
    rb = nbk * TS
    has_prev = c_prev is not None
    grid, blk, lay = _sample_grid(n_seq, c0.shape[0], layer, has_prev)
    rows = lambda width, col: pl.BlockSpec((rb, width), lambda s, i, col=col: (blk(s, i), col))
    vec = lambda width: pl.BlockSpec((1, width), lambda s, i: (0, 0))
    hm = pl.BlockSpec((HEADS, nbk, LANES), lambda s, i: (0, blk(s, i), 0))
    in_specs = [rows(512, C512_MQ), rows(512, C512_MK), rows(1024, C1K_MV), rows(LANES, 0), vec(LANES), vec(D_MODEL),
                pl.BlockSpec((1, nbk, HEADS, DV, DK), lambda s, i: (layer, blk(s, i), 0, 0, 0)),
                hm, rows(HEADS * DK, 0), rows(LANES, 0)]
    args = [p, p, p, gates, bias, gnorm, c0, n0_hm, n_rep, m_rep]
    aliases = {}
    if has_prev:
        in_specs.append(pl.BlockSpec(memory_space=pl.ANY))
        args.append(c_prev)
        aliases[len(args) - 1] = 1
    return pl.pallas_call(
        functools.partial(_mlstm_sample_body, NB=nbk),
        grid=grid,
        in_specs=in_specs,
        out_specs=(rows(D_MODEL, 0),
                   pl.BlockSpec((1, nbk, HEADS, DV, DK), lambda s, i: (lay(s), i, 0, 0, 0)),
                   hm, hm),
        out_shape=(jax.ShapeDtypeStruct((n_seq * TS, D_MODEL), BF16),
                   jax.ShapeDtypeStruct(c0.shape, F32),
                   jax.ShapeDtypeStruct((HEADS, n_seq, LANES), F32),
                   jax.ShapeDtypeStruct((HEADS, n_seq, LANES), F32)),
        input_output_aliases=aliases,
        compiler_params=_cparams(2, 56), name="mlstm_sample",
    )(*args)


def _ref_rows(b, m, L):
    if m >= SUBLANES:
        nb = L // m
        b3 = b.reshape(nb, m, LANES)
        r = b3[:, m // 2 - 1:m // 2, :]
        return jnp.broadcast_to(r, (nb, m, LANES)).reshape(L, LANES)
    bt = b.reshape(L // SUBLANES, SUBLANES, LANES)
    sub = lax.broadcasted_iota(jnp.int32, bt.shape, 1)
    out = None
    for j in range(SUBLANES // m):
        idx = j * m + m // 2 - 1
        rj = jnp.broadcast_to(bt[:, idx:idx + 1, :], bt.shape)
        out = rj if out is None else jnp.where(sub >= j * m, rj, out)
    return out.reshape(L, LANES)


def _pair_level(L):
    rowi = lax.broadcasted_iota(jnp.int32, (L, L), 0)
    coli = lax.broadcasted_iota(jnp.int32, (L, L), 1)
    return jnp.where(rowi > coli, 31 - lax.clz(rowi ^ coli), -1)


def _gla_scores(qb, kb, b2, L, top, level):
    a_mat = jnp.zeros((L, L), F32)
    m = top
    while m >= 2:
        d = b2 - _ref_rows(b2, m, L)
        f = jnp.exp2(jnp.minimum(d, -d)).astype(BF16)
        am = _nt(qb * f, kb * f)
        a_mat = jnp.where(level == (m // 2).bit_length() - 1, am, a_mat)
        m //= 2
    return a_mat


def _gla_chunk(q, k, v, la, states, gnorm, L):
    rowi = lax.broadcasted_iota(jnp.int32, (L, L), 0)
    coli = lax.broadcasted_iota(jnp.int32, (L, L), 1)
    tril = _one_hot(rowi >= coli)
    level = _pair_level(L)
    scale = DK ** -0.5
    ys, new_states = [], []
    for h in range(HEADS):
        s_mat = states[h]
        qb = q[:, h * DK:(h + 1) * DK]
        kb = k[:, h * DK:(h + 1) * DK]
        qf = qb.astype(F32) * scale
        kf = kb.astype(F32)
        vh = v[:, h * DV:(h + 1) * DV]
        b2 = _sel_left2(tril, la[:, h * DK:(h + 1) * DK])
        a_mat = _gla_scores(qb, kb, b2, L, L, level) * scale
        diag = jnp.sum(qf * kf, axis=-1, keepdims=True)
        o = (_nn(a_mat.astype(BF16), vh) + diag * vh.astype(F32)
             + _nn((qf * jnp.exp2(b2)).astype(BF16), s_mat.astype(BF16)))
        ys.append(_head_rmsnorm(o, gnorm[:, h * DV:(h + 1) * DV]))
        b_last = b2[L - 1:L, :]
        kd = (kf * jnp.exp2(b_last - b2)).astype(BF16)
        e_col = jnp.broadcast_to(jnp.exp2(b_last), (SUBLANES, LANES)).T[:, 0:1]
        new_states.append(e_col * s_mat + _tn(kd, vh))
    return ys, new_states


def _gla_log_decay(gates, w2_ref, b2_ref):
    return _log_sigmoid(_nn(gates.astype(BF16), w2_ref[...]) + b2_ref[...]) * (LOG2E / GLA_TAU)


def _gla_prompt_body(q_ref, k_ref, v_ref, g_ref, w2_ref, b2_ref, gn_ref, y_ref, s_ref, *, L, NC):
    @pl.when(pl.program_id(1) == 0)
    def _():
        s_ref[...] = jnp.zeros_like(s_ref)

    la_all = _gla_log_decay(g_ref[...], w2_ref, b2_ref)
    gnorm = gn_ref[...]
    for c in range(NC):
        sl = slice(c * L, (c + 1) * L)
        states = [s_ref[0, h] for h in range(HEADS)]
        ys, new = _gla_chunk(q_ref[sl, :], k_ref[sl, :], v_ref[sl, :], la_all[sl], states, gnorm, L)
        for h in range(HEADS):
            y_ref[sl, h * DV:(h + 1) * DV] = ys[h].astype(BF16)
            s_ref[0, h] = new[h]


def _gla_prompt(p, gates, w2, b2, gnorm, *, nb, t_len):
    tb = MIX_TB
    nblk = t_len // tb
    rows = lambda width, col: pl.BlockSpec((tb, width), lambda b, t, col=col: (b * nblk + t, col))
    vec = lambda width: pl.BlockSpec((1, width), lambda b, t: (0, 0))
    return pl.pallas_call(
        functools.partial(_gla_prompt_body, L=GLA_CHUNK, NC=tb // GLA_CHUNK),
        grid=(nb, nblk),
        in_specs=[rows(512, C512_GQ), rows(512, C512_GK), rows(1024, C1K_GV), rows(LANES, 0),
                  pl.BlockSpec((LANES, 512), lambda b, t: (0, 0)), vec(512), vec(D_MODEL)],
        out_specs=(rows(D_MODEL, 0), pl.BlockSpec((1, HEADS, DK, DV), lambda b, t: (b, 0, 0, 0))),
        out_shape=(jax.ShapeDtypeStruct((nb * t_len, D_MODEL), BF16), jax.ShapeDtypeStruct((nb, HEADS, DK, DV), F32)),
        compiler_params=_cparams(2, 40), name="gla_prompt",
    )(p, p, p, gates, w2, b2, gnorm)


def _gla_sample_body(*refs, NB):
    s_ref = refs[-1]

    @pl.when(pl.program_id(0) == 0)
    def _():
        _gla_sample_compute(*refs, NB=NB)

    @pl.when(pl.program_id(0) != 0)
    def _():
        s_ref[...] = jnp.zeros_like(s_ref)


def _gla_sample_compute(q_ref, k_ref, v_ref, g_ref, w2_ref, b2_ref, gn_ref, s0_ref, *rest, NB):
    y_ref, s_ref = rest[-2:]
    L = NB * TS
    rowi, coli, same, first, _ = _seq_masks(L, NB)
    tril_seq = _one_hot(jnp.logical_and(same, rowi >= coli))
    full_seq = _one_hot(same)
    level = _pair_level(L)
    la_all = _gla_log_decay(g_ref[...], w2_ref, b2_ref)
    gnorm = gn_ref[...]
    q_all = q_ref[...].astype(F32)
    k_all = k_ref[...].astype(F32)
    v_all = v_ref[...].astype(F32)
    upper = lax.broadcasted_iota(jnp.int32, (WIN, 1), 0) < TS
    scale = DK ** -0.5
    for h in range(HEADS):
        qb = q_ref[:, h * DK:(h + 1) * DK]
        kb = k_ref[:, h * DK:(h + 1) * DK]
        qf = q_all[:, h * DK:(h + 1) * DK] * scale
        kf = k_all[:, h * DK:(h + 1) * DK]
        vf = v_all[:, h * DV:(h + 1) * DV]
        vh = vf.astype(BF16)
        la = la_all[:, h * DK:(h + 1) * DK]
        b2 = _sel_left2(tril_seq, la)
        btot = _sel_left2(full_seq, la)
        a_mat = _gla_scores(qb, kb, b2, L, TS, level) * scale
        diag = jnp.sum(qf * kf, axis=-1, keepdims=True)
        qe = qf * jnp.exp2(b2)
        inter = []
        for wd in range(NB // 2):
            qw = qe[WIN * wd:WIN * (wd + 1)].astype(BF16)
            o0 = _nn(qw, s0_ref[0, 2 * wd, h].astype(BF16))
            o1 = _nn(qw, s0_ref[0, 2 * wd + 1, h].astype(BF16))
            inter.append(jnp.where(upper, o0, o1))
        o = _nn(a_mat.astype(BF16), vh) + diag * vf + jnp.concatenate(inter, axis=0)
        y_ref[:, h * DV:(h + 1) * DV] = _head_rmsnorm(o, gnorm[:, h * DV:(h + 1) * DV]).astype(BF16)
        kd = kf * jnp.exp2(btot - b2)
        e_cols = jnp.exp2(_sel_left(first, btot)).T
        for bq in range(NB):
            wd, par = bq // 2, bq % 2
            valid = upper if par == 0 else jnp.logical_not(upper)
            kwin = jnp.where(valid, kd[WIN * wd:WIN * (wd + 1)], 0.0).astype(BF16)
            vwin = vf[WIN * wd:WIN * (wd + 1)].astype(BF16)
            s_ref[0, bq, h] = e_cols[:, bq:bq + 1] * s0_ref[0, bq, h] + _tn(kwin, vwin)


def _gla_sample(p, gates, w2, b2, gnorm, s0, s_prev, *, layer, n_seq):
    nbk = SAMPLE_NB
    rb = nbk * TS
    has_prev = s_prev is not None
    grid, blk, lay = _sample_grid(n_seq, s0.shape[0], layer, has_prev)
    rows = lambda width, col: pl.BlockSpec((rb, width), lambda s, i, col=col: (blk(s, i), col))
    vec = lambda width: pl.BlockSpec((1, width), lambda s, i: (0, 0))
    in_specs = [rows(512, C512_GQ), rows(512, C512_GK), rows(1024, C1K_GV), rows(LANES, 0),
                pl.BlockSpec((LANES, 512), lambda s, i: (0, 0)), vec(512), vec(D_MODEL),
                pl.BlockSpec((1, nbk, HEADS, DK, DV), lambda s, i: (layer, blk(s, i), 0, 0, 0))]
    args = [p, p, p, gates, w2, b2, gnorm, s0]
    aliases = {}
    if has_prev:
        in_specs.append(pl.BlockSpec(memory_space=pl.ANY))
        args.append(s_prev)
        aliases[len(args) - 1] = 1
    return pl.pallas_call(
        functools.partial(_gla_sample_body, NB=nbk),
        grid=grid,
        in_specs=in_specs,
        out_specs=(rows(D_MODEL, 0), pl.BlockSpec((1, nbk, HEADS, DK, DV), lambda s, i: (lay(s), i, 0, 0, 0))),
        out_shape=(jax.ShapeDtypeStruct((n_seq * TS, D_MODEL), BF16), jax.ShapeDtypeStruct(s0.shape, F32)),
        input_output_aliases=aliases,
        compiler_params=_cparams(2, 56), name="gla_sample",
    )(*args)


def _lru_gates(xc, wa_ref, ba_ref, wi_ref, bi_ref, lam_ref):
    ra, ri = [], []
    for nb in range(LRU_BLOCKS):
        xs = xc[:, nb * LRU_BW:(nb + 1) * LRU_BW].astype(BF16)
        ra.append(_nn(xs, wa_ref[nb]))
        ri.append(_nn(xs, wi_ref[nb]))
    r = _sigmoid(jnp.concatenate(ra, axis=-1) + ba_ref[...])
    ig = _sigmoid(jnp.concatenate(ri, axis=-1) + bi_ref[...])
    a = jnp.exp(((-LRU_C) * _softplus(-lam_ref[...])) * r)
    om = 1.0 - a * a
    mult = om * lax.rsqrt(jnp.maximum(om, 1e-30))
    return a, ig * xc, mult


def _conv_shift_matrices(tb):
    nsh = CONV_W - 1
    j = jnp.arange(nsh)[:, None, None] + 1
    t = jnp.arange(tb)[None, :, None]
    s = jnp.arange(tb)[None, None, :]
    shift = (t - s == j).astype(BF16).reshape(nsh * tb, tb)
    t8 = jnp.arange(SUBLANES)[None, :, None]
    s8 = jnp.arange(SUBLANES)[None, None, :]
    tail = (s8 - t8 == SUBLANES - j).astype(BF16).reshape(nsh * SUBLANES, SUBLANES)
    return shift, tail


def _lru_prompt_body(x_ref, shift_ref, tail_ref, wc_ref, bc_ref, wa_ref, ba_ref, wi_ref, bi_ref, lam_ref,
                     h_ref, hfin_ref, cfin_ref, xprev, hcar, *, TB):
    t = pl.program_id(1)

    @pl.when(t == 0)
    def _():
        xprev[...] = jnp.zeros_like(xprev)
        hcar[...] = jnp.zeros_like(hcar)

    xb = x_ref[...]
    x = xb.astype(F32)
    nsh = CONV_W - 1
    shifted = _nn(shift_ref[...], xb)
    from_prev = _nn(tail_ref[...], xprev[...].astype(BF16))
    wc = wc_ref[...]
    xc = bc_ref[...] + wc[CONV_W - 1:CONV_W, :] * x
    for j in range(nsh):
        sj = shifted[j * TB:(j + 1) * TB]
        sj = jnp.concatenate([sj[0:SUBLANES] + from_prev[j * SUBLANES:(j + 1) * SUBLANES], sj[SUBLANES:]], axis=0)
        xc = xc + wc[CONV_W - 2 - j:CONV_W - 1 - j, :] * sj
    xprev[...] = x[TB - SUBLANES:TB, :]

    a, gx, mult = _lru_gates(xc, wa_ref, ba_ref, wi_ref, bi_ref, lam_ref)
    row8 = lax.broadcasted_iota(jnp.int32, (SUBLANES, 1), 0)
    m_first = jnp.where(row8 + t == 0, 1.0, mult[0:SUBLANES])
    u = jnp.concatenate([m_first, mult[SUBLANES:]], axis=0) * gx
    nt8 = TB // SUBLANES
    a3 = a.reshape(nt8, SUBLANES, D_MODEL)
    u3 = u.reshape(nt8, SUBLANES, D_MODEL)
    sub = lax.broadcasted_iota(jnp.int32, (1, SUBLANES, D_MODEL), 1)
    d = 1
    while d < SUBLANES:
        keep = sub >= d
        a_sh = jnp.where(keep, pltpu.roll(a3, d, 1), 1.0)
        u_sh = jnp.where(keep, pltpu.roll(u3, d, 1), 0.0)
        u3 = a3 * u_sh + u3
        a3 = a3 * a_sh
        d *= 2
    h_in = hcar[...]
    hs = []
    for j in range(nt8):
        hj = a3[j] * h_in + u3[j]
        hs.append(hj)
        h_in = hj[SUBLANES - 1:SUBLANES]
    h_ref[...] = jnp.concatenate(hs, axis=0).astype(BF16)
    hcar[...] = h_in
    hfin_ref[0] = h_in
    cfin_ref[0] = x[TB - (CONV_W - 1):TB, :]


def _lru_prompt(p, wc, bc, wa, ba, wi, bi, lam, *, nb, t_len):
    tb = LRU_TB
    nblk = t_len // tb
    vec = pl.BlockSpec((1, D_MODEL), lambda b, t: (0, 0))
    wblk = pl.BlockSpec((LRU_BLOCKS, LRU_BW, LRU_BW), lambda b, t: (0, 0, 0))
    shift, tail = _conv_shift_matrices(tb)
    whole = lambda arr: pl.BlockSpec(arr.shape, lambda b, t: (0, 0))
    return pl.pallas_call(
        functools.partial(_lru_prompt_body, TB=tb),
        grid=(nb, nblk),
        in_specs=[pl.BlockSpec((tb, D_MODEL), lambda b, t: (b * nblk + t, C1K_LX)), whole(shift), whole(tail),
                  pl.BlockSpec((CONV_W, D_MODEL), lambda b, t: (0, 0)), vec, wblk, vec, wblk, vec, vec],
        out_specs=(pl.BlockSpec((tb, D_MODEL), lambda b, t: (b * nblk + t, 0)),
                   pl.BlockSpec((1, 1, D_MODEL), lambda b, t: (b, 0, 0)),
                   pl.BlockSpec((1, CONV_W - 1, D_MODEL), lambda b, t: (b, 0, 0))),
        out_shape=(jax.ShapeDtypeStruct((nb * t_len, D_MODEL), BF16),
                   jax.ShapeDtypeStruct((nb, 1, D_MODEL), F32),
                   jax.ShapeDtypeStruct((nb, CONV_W - 1, D_MODEL), F32)),
        scratch_shapes=[pltpu.VMEM((SUBLANES, D_MODEL), F32), pltpu.VMEM((1, D_MODEL), F32)],
        compiler_params=_cparams(2, 40), name="lru_prompt",
    )(p, shift, tail, wc, bc, wa, ba, wi, bi, lam)


def _lru_sample_body(x_ref, buf_ref, h0_ref, wc_ref, bc_ref, wa_ref, ba_ref, wi_ref, bi_ref, lam_ref,
                     h_ref, hfin_ref, cfin_ref, *, T):
    wc = wc_ref[...]
    xs = [buf_ref[j] for j in range(CONV_W - 1)] + [x_ref[t].astype(F32) for t in range(T)]
    h = h0_ref[...]
    for t in range(T):
        xc = bc_ref[...]
        for j in range(CONV_W):
            xc = xc + wc[j:j + 1, :] * xs[t + j]
        a, gx, mult = _lru_gates(xc, wa_ref, ba_ref, wi_ref, bi_ref, lam_ref)
        h = a * h + mult * gx
        h_ref[t] = h.astype(BF16)
    hfin_ref[...] = h
    for j in range(CONV_W - 1):
        cfin_ref[j] = xs[T + j]


def _lru_sample(x_tm, buf_tm, h0, wc, bc, wa, ba, wi, bi, lam):
    t_len, n_seq, _ = x_tm.shape
    return pl.pallas_call(
        functools.partial(_lru_sample_body, T=t_len),
        out_shape=(jax.ShapeDtypeStruct((t_len, n_seq, D_MODEL), BF16),
                   jax.ShapeDtypeStruct((n_seq, D_MODEL), F32),
                   jax.ShapeDtypeStruct((CONV_W - 1, n_seq, D_MODEL), F32)),
        compiler_params=pltpu.CompilerParams(vmem_limit_bytes=40 * 1024 * 1024), name="lru_sample",
    )(x_tm, buf_tm, h0, wc, bc, wa, ba, wi, bi, lam)


def kernel(x_prompt, x_sample, state_mlstm_C, state_mlstm_n, state_mlstm_m, state_gla_S, state_lru_h, state_lru_conv,
           norm_ffn1, w_ffn1_in, w_ffn1_out, norm_mix, w_in, b_ml_if, g_ml_norm, w_gla_lr2, b_gla_gate, g_gla_norm,
           w_conv, b_conv, w_lru_a, b_lru_a, w_lru_i, b_lru_i, lru_lambda, w_out, norm_ffn2, w_ffn2_in, w_ffn2_out,
           norm_final):
    nb, t_len, d = x_prompt.shape
    n_seq, ts, _ = x_sample.shape
    depth = w_in.shape[0]
    assert d == D_MODEL and ts == TS and t_len % MIX_TB == 0 and n_seq % SAMPLE_NB == 0

    xp = x_prompt.reshape(nb * t_len, d)
    xs = x_sample.reshape(n_seq * ts, d)

    f1_in, f1_out = w_ffn1_in.astype(BF16), w_ffn1_out.astype(BF16)
    f2_in, f2_out = w_ffn2_in.astype(BF16), w_ffn2_out.astype(BF16)
    w_o = w_out.astype(BF16)
    w_in_b = w_in.astype(BF16)
    new_p = {k: [] for k in ("C", "n", "m", "S", "h", "conv")}
    new_s = {k: [] for k in ("n", "m", "h", "conv")}
    s_c_all, s_s_all = None, None
    yp = ys = None
    for l in range(depth):
        bias_if = jnp.concatenate([b_ml_if[l], jnp.zeros((LANES - 2 * HEADS,), F32)]).reshape(1, LANES)
        w2 = jnp.zeros((LANES, HEADS * DK), F32).at[2 * HEADS:2 * HEADS + GLA_RANK].set(w_gla_lr2[l]).astype(BF16)
        b2 = b_gla_gate[l].reshape(1, HEADS * DK)
        g_ml = g_ml_norm[l].reshape(1, d)
        g_gla = g_gla_norm[l].reshape(1, d)
        wc, bc = w_conv[l], b_conv[l].reshape(1, d)
        wa, wi = w_lru_a[l].astype(BF16), w_lru_i[l].astype(BF16)
        ba, bi, lam = b_lru_a[l].reshape(1, d), b_lru_i[l].reshape(1, d), lru_lambda[l].reshape(1, d)
        last = l == depth - 1
        g_next = norm_final if last else norm_ffn1[l + 1]

        x1, xn, gates = _ffn(xp, norm_ffn1[l], f1_in, f1_out, norm_mix[l], layer=l, final=False, w_gate=w_in_b)
        p = _proj(xn, w_in_b, l)
        g_ml_rep = jnp.broadcast_to(g_ml_norm[l][:, None], (d, LANES))
        y_ml, p_c, p_n, p_m = _mlstm_prompt(p, gates, bias_if, g_ml_rep, nb=nb, t_len=t_len)
        y_gla, p_s = _gla_prompt(p, gates, w2, b2, g_gla, nb=nb, t_len=t_len)
        h_lru, p_h, p_conv = _lru_prompt(p, wc, bc, wa, ba, wi, bi, lam, nb=nb, t_len=t_len)
        x2 = _merge(x1, p, y_ml, y_gla, h_lru, w_o, l)
        if last:
            yp = _ffn(x2, norm_ffn2[l], f2_in, f2_out, g_next, layer=l, final=True)
        else:
            xp, _ = _ffn(x2, norm_ffn2[l], f2_in, f2_out, g_next, layer=l, final=False)

        x1, xn, gates = _ffn(xs, norm_ffn1[l], f1_in, f1_out, norm_mix[l], layer=l, final=False, w_gate=w_in_b)
        p = _proj(xn, w_in_b, l)
        n0_hm = state_mlstm_n[l].transpose(1, 0, 2)
        n_rep = jnp.repeat(state_mlstm_n[l].reshape(n_seq, HEADS * DK), ts, axis=0)
        m_rep = jnp.repeat(jnp.pad(state_mlstm_m[l], ((0, 0), (0, LANES - HEADS))), ts, axis=0)
        y_ml, s_c_all, s_n, s_m = _mlstm_sample(p, gates, bias_if, g_ml, state_mlstm_C, n0_hm, n_rep, m_rep,
                                                s_c_all, layer=l, n_seq=n_seq)
        y_gla, s_s_all = _gla_sample(p, gates, w2, b2, g_gla, state_gla_S, s_s_all, layer=l, n_seq=n_seq)
        lx_s = p[:, C1K_LX * 1024:(C1K_LX + 1) * 1024].reshape(n_seq, ts, d).transpose(1, 0, 2)
        hs_tm, s_h, s_conv_tm = _lru_sample(lx_s, state_lru_conv[l].transpose(1, 0, 2), state_lru_h[l],
                                            wc, bc, wa, ba, wi, bi, lam)
        h_lru = hs_tm.transpose(1, 0, 2).reshape(n_seq * ts, d)
        x2 = _merge(x1, p, y_ml, y_gla, h_lru, w_o, l)
        if last:
            ys = _ffn(x2, norm_ffn2[l], f2_in, f2_out, g_next, layer=l, final=True)
        else:
            xs, _ = _ffn(x2, norm_ffn2[l], f2_in, f2_out, g_next, layer=l, final=False)

        new_p["C"].append(p_c)
        new_p["n"].append(p_n)
        new_p["m"].append(p_m[:, :, 0])
        new_p["S"].append(p_s)
        new_p["h"].append(p_h[:, 0, :])
        new_p["conv"].append(p_conv)
        new_s["n"].append(s_n.transpose(1, 0, 2))
        new_s["m"].append(s_m[:, :, 0].T)
        new_s["h"].append(s_h)
        new_s["conv"].append(s_conv_tm.transpose(1, 0, 2))

    st = lambda arrs: jnp.stack(arrs)
    return (yp.reshape(nb, t_len, d), ys.reshape(n_seq, ts, d),
            st(new_p["C"]), st(new_p["n"]), st(new_p["m"]), st(new_p["S"]), st(new_p["h"]), st(new_p["conv"]),
            s_c_all, st(new_s["n"]), st(new_s["m"]), s_s_all, st(new_s["h"]), st(new_s["conv"]))
```
